```python
import math
import jax, jax.numpy as jnp
from jax import lax
import numpy as np

D_MODEL = 1024
BATCH = 16
SEQ = 2048
DEPTH = 2

D_MIX = D_MODEL
D_GMLP = D_MIX // 2
D_MLA = D_MIX - D_GMLP
GMLP_GROUPS = 8
GMLP_GROUP_DIM = D_GMLP // GMLP_GROUPS
CHUNK = 128
MLA_HEADS = 8
QK_NOPE_DIM = 64
QK_ROPE_DIM = 32
V_HEAD_DIM = D_MLA // MLA_HEADS
Q_RANK = D_MODEL // 4
KV_RANK = D_MODEL // 8
ROPE_THETA = 10000.0
Q_BLOCK = 128
D_FF = 4 * D_MODEL
N_MOD = 6
EPS = 1e-6
D_IN = 2 * D_GMLP + Q_RANK + KV_RANK + QK_ROPE_DIM

kernel_name = "hybrid_gmlp_mla_adaln_block"


def rmsnorm(x, g):
    xf = x.astype(jnp.float32)
    y = xf * lax.rsqrt(jnp.mean(xf * xf, axis=-1, keepdims=True) + EPS)
    return (y * g.astype(jnp.float32)).astype(x.dtype)


def layernorm_noaffine(x):
    xf = x.astype(jnp.float32)
    mu = jnp.mean(xf, axis=-1, keepdims=True)
    d = xf - mu
    y = d * lax.rsqrt(jnp.mean(d * d, axis=-1, keepdims=True) + EPS)
    return y.astype(x.dtype)


def rope_tables(positions, dim):
    freqs = ROPE_THETA ** (-jnp.arange(0, dim, 2, dtype=jnp.float32) / dim)
    ang = positions.astype(jnp.float32)[..., None] * freqs
    return jnp.cos(ang), jnp.sin(ang)


def apply_rope(x, cos, sin):
    half = x.shape[-1] // 2
    x1, x2 = x[..., :half], x[..., half:]
    cos = cos.astype(x.dtype)
    sin = sin.astype(x.dtype)
    return jnp.concatenate([x1 * cos - x2 * sin, x1 * sin + x2 * cos], axis=-1)


def gmlp_mixer(u, v, w_s, b_s):
    B, S, G, Dg = u.shape
    n_chunks = S // CHUNK
    u = jax.nn.gelu(u)
    v = layernorm_noaffine(jax.nn.gelu(v))
    causal = jnp.tril(jnp.ones((CHUNK, CHUNK), dtype=bool))
    w = jnp.where(causal[None], w_s, 0.0)
    vc = v.reshape(B, n_chunks, CHUNK, G, Dg)
    mixed = jnp.einsum('gts,bcsgd->bctgd', w, vc) + b_s.T[None, None, :, :, None]
    return u * mixed.reshape(B, S, G, Dg)


def mla_mixer(q_lat, kv_lat, k_rope_raw, cos, sin, g_q, g_kv, w_uq, w_ukv):
    B, S, _ = q_lat.shape
    c_q = rmsnorm(q_lat, g_q)
    q = (c_q @ w_uq).reshape(B, S, MLA_HEADS, QK_NOPE_DIM + QK_ROPE_DIM)
    q_nope, q_rope = q[..., :QK_NOPE_DIM], q[..., QK_NOPE_DIM:]
    q_rope = apply_rope(q_rope, cos[:, :, None, :], sin[:, :, None, :])
    c_kv = rmsnorm(kv_lat, g_kv)
    kv = (c_kv @ w_ukv).reshape(B, S, MLA_HEADS, QK_NOPE_DIM + V_HEAD_DIM)
    k_nope, v = kv[..., :QK_NOPE_DIM], kv[..., QK_NOPE_DIM:]
    k_rope = apply_rope(k_rope_raw, cos, sin)
    scale = (QK_NOPE_DIM + QK_ROPE_DIM) ** -0.5
    outs = []
    for i in range(S // Q_BLOCK):
        q0 = i * Q_BLOCK
        kend = q0 + Q_BLOCK
        s = (jnp.einsum('bqhd,bkhd->bhqk', q_nope[:, q0:kend], k_nope[:, :kend])
             + jnp.einsum('bqhr,bkr->bhqk', q_rope[:, q0:kend], k_rope[:, :kend]))
        s = s.astype(jnp.float32) * scale
        qpos = q0 + jnp.arange(Q_BLOCK)
        kpos = jnp.arange(kend)
        s = jnp.where(kpos[None, :] <= qpos[:, None], s, -1e30)
        p = jax.nn.softmax(s, axis=-1).astype(v.dtype)
        outs.append(jnp.einsum('bhqk,bkhd->bqhd', p, v[:, :kend]))
    o = jnp.concatenate(outs, axis=1)
    return o.reshape(B, S, MLA_HEADS * V_HEAD_DIM)


def setup_inputs(seed: int = 0) -> dict:
    key = jax.random.key(seed)
    ks = jax.random.split(key, 24)
    f32 = jnp.float32

    def nrm(k, shape, scale):
        return jax.random.normal(k, shape, f32) * scale

    def gain(k, shape):
        return 1.0 + 0.02 * jax.random.normal(k, shape, f32)

    x = jax.random.normal(ks[0], (BATCH, SEQ, D_MODEL), f32)
    c = jax.random.normal(ks[1], (BATCH, D_MODEL), f32)
    offset = jax.random.randint(ks[2], (BATCH, 1), 0, 1024, dtype=jnp.int32)
    positions = offset + jnp.arange(SEQ, dtype=jnp.int32)[None, :]
    return {
        "x": x,
        "c": c,
        "positions": positions,
        "w_ada": nrm(ks[3], (DEPTH, D_MODEL, N_MOD * D_MODEL), 0.02),
        "b_ada": nrm(ks[4], (DEPTH, N_MOD * D_MODEL), 0.01),
        "norm_mix_g": gain(ks[5], (DEPTH, D_MODEL)),
        "w_in": nrm(ks[6], (DEPTH, D_MODEL, D_IN), D_MODEL ** -0.5),
        "gmlp_ws": nrm(ks[7], (DEPTH, GMLP_GROUPS, CHUNK, CHUNK), CHUNK ** -0.5),
        "gmlp_bs": gain(ks[8], (DEPTH, GMLP_GROUPS, CHUNK)),
        "mla_q_norm_g": gain(ks[9], (DEPTH, Q_RANK)),
        "mla_kv_norm_g": gain(ks[10], (DEPTH, KV_RANK)),
        "mla_w_uq": nrm(ks[11], (DEPTH, Q_RANK, MLA_HEADS * (QK_NOPE_DIM + QK_ROPE_DIM)), Q_RANK ** -0.5),
        "mla_w_ukv": nrm(ks[12], (DEPTH, KV_RANK, MLA_HEADS * (QK_NOPE_DIM + V_HEAD_DIM)), KV_RANK ** -0.5),
        "out_norm_gmlp_g": gain(ks[13], (DEPTH, D_GMLP)),
        "out_norm_mla_g": gain(ks[14], (DEPTH, D_MLA)),
        "w_out": nrm(ks[15], (DEPTH, D_MIX, D_MODEL), D_MIX ** -0.5),
        "norm_ffn_g": gain(ks[16], (DEPTH, D_MODEL)),
        "w_ff1": nrm(ks[17], (DEPTH, D_MODEL, D_FF), D_MODEL ** -0.5),
        "w_ff2": nrm(ks[18], (DEPTH, D_FF, D_MODEL), D_FF ** -0.5),
        "final_norm_g": gain(ks[19], (D_MODEL,)),
    }


def reference(x, c, positions, w_ada, b_ada, norm_mix_g, w_in, gmlp_ws, gmlp_bs,
              mla_q_norm_g, mla_kv_norm_g, mla_w_uq, mla_w_ukv, out_norm_gmlp_g,
              out_norm_mla_g, w_out, norm_ffn_g, w_ff1, w_ff2, final_norm_g):
    B, S, _ = x.shape
    cos, sin = rope_tables(positions, QK_ROPE_DIM)
    c_act = jax.nn.silu(c)
    split_pts = [D_GMLP, 2 * D_GMLP, 2 * D_GMLP + Q_RANK, 2 * D_GMLP + Q_RANK + KV_RANK]
    for l in range(DEPTH):
        mod = c_act @ w_ada[l] + b_ada[l]
        shift1, scale1, gate1, shift2, scale2, gate2 = jnp.split(mod[:, None, :], N_MOD, axis=-1)

        h = rmsnorm(x, norm_mix_g[l]) * (1.0 + scale1) + shift1
        z = h @ w_in[l]
        u, v, q_lat, kv_lat, k_rope_raw = jnp.split(z, split_pts, axis=-1)
        y_g = gmlp_mixer(u.reshape(B, S, GMLP_GROUPS, GMLP_GROUP_DIM),
                         v.reshape(B, S, GMLP_GROUPS, GMLP_GROUP_DIM),
                         gmlp_ws[l], gmlp_bs[l]).reshape(B, S, D_GMLP)
        y_a = mla_mixer(q_lat, kv_lat, k_rope_raw, cos, sin, mla_q_norm_g[l],
                        mla_kv_norm_g[l], mla_w_uq[l], mla_w_ukv[l])
        y = jnp.concatenate([rmsnorm(y_g, out_norm_gmlp_g[l]), rmsnorm(y_a, out_norm_mla_g[l])], axis=-1)
        x = x + gate1 * (y @ w_out[l])

        h = rmsnorm(x, norm_ffn_g[l]) * (1.0 + scale2) + shift2
        f = jnp.square(jax.nn.relu(h @ w_ff1[l])) @ w_ff2[l]
        x = x + gate2 * f
    return rmsnorm(x, final_norm_g)
```

```python
import functools
import math

import jax
import jax.numpy as jnp
from jax import lax
from jax.experimental import pallas as pl
from jax.experimental.pallas import tpu as pltpu

F32 = jnp.float32
BF16 = jnp.bfloat16

EPS = 1e-6
ROPE_THETA = 10000.0
N_MOD = 6

GMLP_GROUPS = 8
CHUNK = 128
MLA_HEADS = 8
QK_NOPE_DIM = 64
QK_ROPE_DIM = 32
V_HEAD_DIM = 64
HEAD_SLAB = 128
LANES = 128

VMEM_LIMIT_BYTES = 56 * 1024 * 1024

TM_IN = 512
TM_OUT = 512
TQ = 256
FF_CHUNK = 1024
ADA_TN = 1536


def _dot(a, b):
    return jnp.dot(a, b, preferred_element_type=F32)


def _rms(x):
    return x * lax.rsqrt(jnp.mean(x * x, axis=-1, keepdims=True) + EPS)


def _gelu_tanh(x):
    c = math.sqrt(2.0 / math.pi)
    return 0.5 * x * (1.0 + jnp.tanh(c * (x + 0.044715 * (x * x * x))))


def _adaln_kernel(c_ref, w_ref, b_ref, o_ref):
    c = c_ref[...]
    c_act = (c / (1.0 + jnp.exp(-c))).astype(BF16)
    o_ref[0] = _dot(c_act, w_ref[0].astype(BF16)) + b_ref[0]


def _adaln(c, w_ada, b_ada):
    depth, d, n = w_ada.shape
    b = c.shape[0]
    return pl.pallas_call(
        _adaln_kernel,
        grid=(depth, n // ADA_TN),
        in_specs=[
            pl.BlockSpec((b, d), lambda l, j: (0, 0)),
            pl.BlockSpec((1, d, ADA_TN), lambda l, j: (l, 0, j)),
            pl.BlockSpec((1, 1, ADA_TN), lambda l, j: (l, 0, j)),
        ],
        out_specs=pl.BlockSpec((1, b, ADA_TN), lambda l, j: (l, 0, j)),
        out_shape=jax.ShapeDtypeStruct((depth, b, n), F32),
        compiler_params=pltpu.CompilerParams(
            dimension_semantics=("arbitrary", "arbitrary"),
            vmem_limit_bytes=VMEM_LIMIT_BYTES),
        name="adaln_mod",
    )(c, w_ada, b_ada.reshape(depth, 1, n))


def _rope_kernel(pos_ref, freq_ref, cos_ref, sin_ref):
    pos = pos_ref[0].astype(F32)
    ang = pos * freq_ref[...]
    lane = lax.broadcasted_iota(jnp.int32, ang.shape, 1)
    rope = (lane >= QK_NOPE_DIM) & (lane < QK_NOPE_DIM + QK_ROPE_DIM)
    cos_ref[0] = jnp.where(rope, jnp.cos(ang), jnp.where(lane < QK_NOPE_DIM, 1.0, 0.0))
    sin_ref[0] = jnp.where(rope, jnp.sin(ang), 0.0)


def _rope_tables(positions):
    b, s = positions.shape
    half = QK_ROPE_DIM // 2
    freqs = ROPE_THETA ** (-jnp.arange(0, QK_ROPE_DIM, 2, dtype=F32) / QK_ROPE_DIM)
    freq_lane = jnp.concatenate(
        [jnp.zeros((QK_NOPE_DIM,), F32), freqs, freqs,
         jnp.zeros((HEAD_SLAB - QK_NOPE_DIM - 2 * half,), F32)]).reshape(1, HEAD_SLAB)
    ts = 512
    out = jax.ShapeDtypeStruct((b, s, HEAD_SLAB), F32)
    return pl.pallas_call(
        _rope_kernel,
        grid=(b, s // ts),
        in_specs=[
            pl.BlockSpec((1, ts, 1), lambda i, j: (i, j, 0)),
            pl.BlockSpec((1, HEAD_SLAB), lambda i, j: (0, 0)),
        ],
        out_specs=[pl.BlockSpec((1, ts, HEAD_SLAB), lambda i, j: (i, j, 0))] * 2,
        out_shape=[out, out],
        compiler_params=pltpu.CompilerParams(
            dimension_semantics=("arbitrary", "arbitrary")),
        name="rope_tables",
    )(positions.reshape(b, s, 1), freq_lane)


def _mix_in_kernel(x_ref, mod_ref, cos_ref, sin_ref, g_mix_ref, w_in_ref, ws_ref,
                   bias_ref, ln_ref, g_q_ref, g_kv_ref, w_uq_ref, w_ukv_ref, g_og_ref,
                   yg_ref, q_ref, k_ref, v_ref, *, d_gmlp, q_rank, kv_rank, q_scale):
    x = x_ref[0]
    tm = x.shape[0]
    mod = mod_ref[0, 0]
    shift1, scale1 = mod[0:1], mod[1:2]
    h = (_rms(x) * g_mix_ref[...]) * (1.0 + scale1) + shift1
    z = _dot(h.astype(BF16), w_in_ref[...])

    o_q = 2 * d_gmlp
    o_kv = o_q + q_rank
    o_kr = o_kv + kv_rank
    gu = _gelu_tanh(z[:, :d_gmlp])
    gv = _gelu_tanh(z[:, d_gmlp:o_q])
    ln_m = ln_ref[...]

    def group_mean(a):
        hi = a.astype(BF16)
        lo = (a - hi.astype(F32)).astype(BF16)
        return _dot(hi, ln_m) + _dot(lo, ln_m)

    dv = gv - group_mean(gv)
    vn = (dv * lax.rsqrt(group_mean(dv * dv) + EPS)).astype(BF16)

    row = lax.broadcasted_iota(jnp.int32, (CHUNK, CHUNK), 0)
    col = lax.broadcasted_iota(jnp.int32, (CHUNK, CHUNK), 1)
    w_tril = [jnp.where(row >= col, ws_ref[g], 0.0).astype(BF16) for g in range(GMLP_GROUPS)]
    low_half = col < (LANES // 2)
    bias = bias_ref[...]
    mixed_rows = []
    for c in range(tm // CHUNK):
        vc = vn[c * CHUNK:(c + 1) * CHUNK]
        pieces = []
        for j in range(d_gmlp // LANES):
            vp = vc[:, j * LANES:(j + 1) * LANES]
            pieces.append(jnp.where(low_half, _dot(w_tril[2 * j], vp), _dot(w_tril[2 * j + 1], vp)))
        mixed_rows.append(jnp.concatenate(pieces, axis=1) + bias)
    yg = gu * jnp.concatenate(mixed_rows, axis=0)
    yg_ref[0] = (_rms(yg) * g_og_ref[...]).astype(BF16)

    cos_t = cos_ref[0]
    sin_t = sin_ref[0]
    shift = HEAD_SLAB - QK_ROPE_DIM

    c_q = (_rms(z[:, o_q:o_kv]) * g_q_ref[...]).astype(BF16)
    qf = _dot(c_q, w_uq_ref[...])
    for hd in range(MLA_HEADS):
        qs = qf[:, hd * HEAD_SLAB:(hd + 1) * HEAD_SLAB]
        q_ref[0, hd] = ((qs * cos_t + pltpu.roll(qs, shift, 1) * sin_t) * q_scale).astype(BF16)

    c_kv = (_rms(z[:, o_kv:o_kr]) * g_kv_ref[...]).astype(BF16)
    kvf = _dot(c_kv, w_ukv_ref[...])
    kr = z[:, o_kr:]
    kr = kr * cos_t + pltpu.roll(kr, shift, 1) * sin_t
    for hd in range(MLA_HEADS):
        k_ref[0, hd] = (kvf[:, hd * HEAD_SLAB:(hd + 1) * HEAD_SLAB] + kr).astype(BF16)
    v_ref[0] = kvf[:, MLA_HEADS * HEAD_SLAB:].astype(BF16)


def _mix_in(x, mod, cos_t, sin_t, g_mix, w_in, ws, bias, ln_m, g_q, g_kv, w_uq, w_ukv, g_og):
    b, s, d = x.shape
    d_gmlp = g_og.shape[-1]
    q_rank = g_q.shape[-1]
    kv_rank = g_kv.shape[-1]
    d_v = MLA_HEADS * V_HEAD_DIM
    tm = TM_IN
    q_scale = (QK_NOPE_DIM + QK_ROPE_DIM) ** -0.5 * math.log2(math.e)

    def full(a):
        return pl.BlockSpec(a.shape, lambda i, j: (0,) * a.ndim)

    kern = functools.partial(_mix_in_kernel, d_gmlp=d_gmlp, q_rank=q_rank,
                             kv_rank=kv_rank, q_scale=q_scale)
    slab = jax.ShapeDtypeStruct((b, MLA_HEADS, s, HEAD_SLAB), BF16)
    return pl.pallas_call(
        kern,
        grid=(b, s // tm),
        in_specs=[
            pl.BlockSpec((1, tm, d), lambda i, j: (i, j, 0)),
            pl.BlockSpec((1, 1, N_MOD, d), lambda i, j: (0, i, 0, 0)),
            pl.BlockSpec((1, tm, HEAD_SLAB), lambda i, j: (i, j, 0)),
            pl.BlockSpec((1, tm, HEAD_SLAB), lambda i, j: (i, j, 0)),
            full(g_mix), full(w_in), full(ws), full(bias), full(ln_m),
            full(g_q), full(g_kv), full(w_uq), full(w_ukv), full(g_og),
        ],
        out_specs=[
            pl.BlockSpec((1, tm, d_gmlp), lambda i, j: (i, j, 0)),
            pl.BlockSpec((1, MLA_HEADS, tm, HEAD_SLAB), lambda i, j: (i, 0, j, 0)),
            pl.BlockSpec((1, MLA_HEADS, tm, HEAD_SLAB), lambda i, j: (i, 0, j, 0)),
            pl.BlockSpec((1, tm, d_v), lambda i, j: (i, j, 0)),
        ],
        out_shape=[
            jax.ShapeDtypeStruct((b, s, d_gmlp), BF16),
            slab, slab,
            jax.ShapeDtypeStruct((b, s, d_v), BF16),
        ],
        compiler_params=pltpu.CompilerParams(
            dimension_semantics=("arbitrary", "arbitrary"),
            vmem_limit_bytes=VMEM_LIMIT_BYTES),
        name="mix_in",
    )(x, mod, cos_t, sin_t, g_mix, w_in, ws, bias, ln_m, g_q, g_kv, w_uq, w_ukv, g_og)


def _attn_kernel(q_ref, k_ref, v_ref, o_ref, *, tq):
    i = pl.program_id(2)
    nt = (((1,), (1,)), ((), ()))
    q = [q_ref[0, e] for e in range(2)]

    def step(koff, carry, masked):
        v_blk = v_ref[0, pl.ds(koff, tq), :]
        new = []
        for e in range(2):
            m, l, acc = carry[e]
            s = lax.dot_general(q[e], k_ref[0, e, pl.ds(koff, tq), :], nt,
                                preferred_element_type=F32)
            if masked:
                r = lax.broadcasted_iota(jnp.int32, s.shape, 0)
                c = lax.broadcasted_iota(jnp.int32, s.shape, 1)
                s = jnp.where(c <= r, s, -1e30)
            m_new = jnp.maximum(m, jnp.max(s, axis=-1, keepdims=True))
            alpha = jnp.exp2(m - m_new)
            p = jnp.exp2(s - m_new)
            l = alpha * l + jnp.sum(p, axis=-1, keepdims=True)
            acc = alpha * acc + _dot(p.astype(BF16), v_blk)
            new.append((m_new, l, acc))
        return tuple(new)

    init = tuple((jnp.full((tq, 1), -1e30, F32), jnp.zeros((tq, 1), F32),
                  jnp.zeros((tq, LANES), F32)) for _ in range(2))
    carry = lax.fori_loop(
        0, i, lambda kb, cr: step(pl.multiple_of(kb * tq, tq), cr, False), init)
    carry = step(pl.multiple_of(i * tq, tq), carry, True)
    (_, l0, a0), (_, l1, a1) = carry
    lane = lax.broadcasted_iota(jnp.int32, a0.shape, 1)
    o_ref[0] = jnp.where(lane < V_HEAD_DIM, a0 * (1.0 / l0), a1 * (1.0 / l1)).astype(o_ref.dtype)


def _attention(q, k, v):
    b, hds, s, _ = q.shape
    tq = TQ
    return pl.pallas_call(
        functools.partial(_attn_kernel, tq=tq),
        grid=(b, hds // 2, s // tq),
        in_specs=[
            pl.BlockSpec((1, 2, tq, HEAD_SLAB), lambda bi, j, i: (bi, j, i, 0)),
            pl.BlockSpec((1, 2, s, HEAD_SLAB), lambda bi, j, i: (bi, j, 0, 0)),
            pl.BlockSpec((1, s, LANES), lambda bi, j, i: (bi, 0, j)),
        ],
        out_specs=pl.BlockSpec((1, tq, LANES), lambda bi, j, i: (bi, i, j)),
        out_shape=jax.ShapeDtypeStruct(v.shape, BF16),
        compiler_params=pltpu.CompilerParams(
            dimension_semantics=("arbitrary", "arbitrary", "arbitrary"),
            vmem_limit_bytes=VMEM_LIMIT_BYTES),
        name="attn",
    )(q, k, v)


def _mix_out_kernel(x_ref, yg_ref, ya_ref, mod_ref, g_oa_ref, w_out_ref, g_ffn_ref,
                    w1_ref, w2_ref, g_fin_ref, o_ref, *, final):
    x = x_ref[0]
    mod = mod_ref[0, 0]
    gate1, shift2, scale2, gate2 = mod[2:3], mod[3:4], mod[4:5], mod[5:6]
    d_g = yg_ref.shape[-1]
    ya = (_rms(ya_ref[0].astype(F32)) * g_oa_ref[...]).astype(BF16)
    mixed = _dot(yg_ref[0], w_out_ref[:d_g, :]) + _dot(ya, w_out_ref[d_g:, :])
    x1 = x + gate1 * mixed

    h = ((_rms(x1) * g_ffn_ref[...]) * (1.0 + scale2) + shift2).astype(BF16)
    f = jnp.zeros_like(x1)
    for c in range(w1_ref.shape[1] // FF_CHUNK):
        a = jnp.maximum(_dot(h, w1_ref[:, c * FF_CHUNK:(c + 1) * FF_CHUNK]), 0.0)
        f = f + _dot((a * a).astype(BF16), w2_ref[c * FF_CHUNK:(c + 1) * FF_CHUNK, :])
    x2 = x1 + gate2 * f
    if final:
        x2 = _rms(x2) * g_fin_ref[...]
    o_ref[0] = x2


def _mix_out(x, yg, ya, mod, g_oa, w_out, g_ffn, w1, w2, g_fin, final):
    b, s, d = x.shape
    tm = TM_OUT

    def full(a):
        return pl.BlockSpec(a.shape, lambda i, j: (0,) * a.ndim,
                            pipeline_mode=pl.Buffered(1))

    def tok(a):
        return pl.BlockSpec((1, tm, a.shape[-1]), lambda i, j: (i, j, 0))

    return pl.pallas_call(
        functools.partial(_mix_out_kernel, final=final),
        grid=(b, s // tm),
        in_specs=[
            tok(x), tok(yg), tok(ya),
            pl.BlockSpec((1, 1, N_MOD, d), lambda i, j: (0, i, 0, 0)),
            full(g_oa), full(w_out), full(g_ffn), full(w1), full(w2), full(g_fin),
        ],
        out_specs=tok(x),
        out_shape=jax.ShapeDtypeStruct(x.shape, F32),
        compiler_params=pltpu.CompilerParams(
            dimension_semantics=("arbitrary", "arbitrary"),
            vmem_limit_bytes=VMEM_LIMIT_BYTES),
        name="mix_out_final" if final else "mix_out",
    )(x, yg, ya, mod, g_oa, w_out, g_ffn, w1, w2, g_fin)


def _rotate_half_cols(w):
    half = w.shape[-1] // 2
    return jnp.concatenate([-w[..., half:], w[..., :half]], axis=-1)


def _prep_weights(w_in, mla_w_uq, mla_w_ukv, gmlp_bs):
    depth, d, _ = w_in.shape
    q_rank = mla_w_uq.shape[1]
    kv_rank = mla_w_ukv.shape[1]
    o_kr = w_in.shape[-1] - QK_ROPE_DIM
    kr = w_in[..., o_kr:]
    w_in_ext = jnp.concatenate(
        [w_in[..., :o_kr], jnp.zeros((depth, d, QK_NOPE_DIM), F32), kr, _rotate_half_cols(kr)],
        axis=-1).astype(BF16)

    wq = mla_w_uq.reshape(depth, q_rank, MLA_HEADS, QK_NOPE_DIM + QK_ROPE_DIM)
    rope = wq[..., QK_NOPE_DIM:]
    w_uq_ext = jnp.concatenate([wq, _rotate_half_cols(rope)], axis=-1)
    w_uq_ext = w_uq_ext.reshape(depth, q_rank, MLA_HEADS * HEAD_SLAB).astype(BF16)

    wkv = mla_w_ukv.reshape(depth, kv_rank, MLA_HEADS, QK_NOPE_DIM + V_HEAD_DIM)
    w_k = jnp.concatenate(
        [wkv[..., :QK_NOPE_DIM],
         jnp.zeros((depth, kv_rank, MLA_HEADS, HEAD_SLAB - QK_NOPE_DIM), F32)], axis=-1)
    w_ukv_ext = jnp.concatenate(
        [w_k.reshape(depth, kv_rank, MLA_HEADS * HEAD_SLAB),
         wkv[..., QK_NOPE_DIM:].reshape(depth, kv_rank, MLA_HEADS * V_HEAD_DIM)],
        axis=-1).astype(BF16)

    return w_in_ext, w_uq_ext, w_ukv_ext


def kernel(x, c, positions, w_ada, b_ada, norm_mix_g, w_in, gmlp_ws, gmlp_bs, mla_q_norm_g,
           mla_kv_norm_g, mla_w_uq, mla_w_ukv, out_norm_gmlp_g, out_norm_mla_g, w_out,
           norm_ffn_g, w_ff1, w_ff2, final_norm_g):
    b, s, d = x.shape
    depth = w_ada.shape[0]
    d_gmlp = out_norm_gmlp_g.shape[-1]
    group_dim = d_gmlp // GMLP_GROUPS

    mod = _adaln(c, w_ada, b_ada).reshape(depth, b, N_MOD, d)
    cos_t, sin_t = _rope_tables(positions)
    w_in_ext, w_uq_ext, w_ukv_ext = _prep_weights(w_in, mla_w_uq, mla_w_ukv, gmlp_bs)
    bias = jnp.repeat(jnp.swapaxes(gmlp_bs, 1, 2), group_dim, axis=-1)
    grp = jnp.arange(d_gmlp) // group_dim
    ln_m = jnp.where(grp[:, None] == grp[None, :], 1.0 / group_dim, 0.0).astype(BF16)
    w_out_b = w_out.astype(BF16)
    w1_b = w_ff1.astype(BF16)
    w2_b = w_ff2.astype(BF16)
    g_fin = final_norm_g.reshape(1, d)

    for l in range(depth):
        yg, q, k, v = _mix_in(
            x, mod[l:l + 1], cos_t, sin_t, norm_mix_g[l:l + 1], w_in_ext[l], gmlp_ws[l],
            bias[l], ln_m, mla_q_norm_g[l:l + 1], mla_kv_norm_g[l:l + 1], w_uq_ext[l],
            w_ukv_ext[l], out_norm_gmlp_g[l:l + 1])
        ya = _attention(q, k, v)
        x = _mix_out(x, yg, ya, mod[l:l + 1], out_norm_mla_g[l:l + 1], w_out_b[l],
                     norm_ffn_g[l:l + 1], w1_b[l], w2_b[l], g_fin, final=(l == depth - 1))
    return x
```

```python
import functools
import math

import jax
import jax.numpy as jnp
from jax import lax
from jax.experimental import pallas as pl
from jax.experimental.pallas import tpu as pltpu

F32 = jnp.float32
BF16 = jnp.bfloat16

EPS = 1e-6
ROPE_THETA = 10000.0
N_MOD = 6

GMLP_GROUPS = 8
CHUNK = 128
MLA_HEADS = 8
QK_NOPE_DIM = 64
QK_ROPE_DIM = 32
V_HEAD_DIM = 64
HEAD_SLAB = 128
LANES = 128

VMEM_LIMIT_BYTES = 56 * 1024 * 1024

TM_IN = 512
TM_OUT = 512
TQ = 256
FF_CHUNK = 1024
ADA_TN = 1536


def _dot(a, b):
    return jnp.dot(a, b, preferred_element_type=F32)


def _rms(x):
    return x * lax.rsqrt(jnp.mean(x * x, axis=-1, keepdims=True) + EPS)


def _gelu_tanh(x):
    c = math.sqrt(2.0 / math.pi)
    return 0.5 * x * (1.0 + jnp.tanh(c * (x + 0.044715 * (x * x * x))))


def _adaln_kernel(c_ref, w_ref, b_ref, o_ref):
    c = c_ref[...]
    c_act = (c / (1.0 + jnp.exp(-c))).astype(BF16)
    o_ref[0] = _dot(c_act, w_ref[0].astype(BF16)) + b_ref[0]


def _adaln(c, w_ada, b_ada):
    depth, d, n = w_ada.shape
    b = c.shape[0]
    return pl.pallas_call(
        _adaln_kernel,
        grid=(depth, n // ADA_TN),
        in_specs=[
            pl.BlockSpec((b, d), lambda l, j: (0, 0)),
            pl.BlockSpec((1, d, ADA_TN), lambda l, j: (l, 0, j)),
            pl.BlockSpec((1, 1, ADA_TN), lambda l, j: (l, 0, j)),
        ],
        out_specs=pl.BlockSpec((1, b, ADA_TN), lambda l, j: (l, 0, j)),
        out_shape=jax.ShapeDtypeStruct((depth, b, n), F32),
        compiler_params=pltpu.CompilerParams(
            dimension_semantics=("arbitrary", "arbitrary"),
            vmem_limit_bytes=VMEM_LIMIT_BYTES),
        name="adaln_mod",
    )(c, w_ada, b_ada.reshape(depth, 1, n))


def _rope_kernel(pos_ref, freq_ref, cos_ref, sin_ref):
    pos = pos_ref[0].astype(F32)
    ang = pos * freq_ref[...]
    lane = lax.broadcasted_iota(jnp.int32, ang.shape, 1)
    rope = (lane >= QK_NOPE_DIM) & (lane < QK_NOPE_DIM + QK_ROPE_DIM)
    cos_ref[0] = jnp.where(rope, jnp.cos(ang), jnp.where(lane < QK_NOPE_DIM, 1.0, 0.0))
    sin_ref[0] = jnp.where(rope, jnp.sin(ang), 0.0)


def _rope_tables(positions):
    b, s = positions.shape
    half = QK_ROPE_DIM // 2
    freqs = ROPE_THETA ** (-jnp.arange(0, QK_ROPE_DIM, 2, dtype=F32) / QK_ROPE_DIM)
    freq_lane = jnp.concatenate(
        [jnp.zeros((QK_NOPE_DIM,), F32), freqs, freqs,
         jnp.zeros((HEAD_SLAB - QK_NOPE_DIM - 2 * half,), F32)]).reshape(1, HEAD_SLAB)
    ts = 512
    out = jax.ShapeDtypeStruct((b, s, HEAD_SLAB), F32)
    return pl.pallas_call(
        _rope_kernel,
        grid=(b, s // ts),
        in_specs=[
            pl.BlockSpec((1, ts, 1), lambda i, j: (i, j, 0)),
            pl.BlockSpec((1, HEAD_SLAB), lambda i, j: (0, 0)),
        ],
        out_specs=[pl.BlockSpec((1, ts, HEAD_SLAB), lambda i, j: (i, j, 0))] * 2,
        out_shape=[out, out],
        compiler_params=pltpu.CompilerParams(
            dimension_semantics=("arbitrary", "arbitrary")),
        name="rope_tables",
    )(positions.reshape(b, s, 1), freq_lane)


def _mix_in_kernel(x_ref, mod_ref, cos_ref, sin_ref, g_mix_ref, w_in_ref, ws_ref,
                   bias_ref, ln_ref, g_q_ref, g_kv_ref, w_uq_ref, w_uk_ref, w_uvt_ref, g_og_ref,
                   yg_ref, q_ref, k_ref, vt_ref, *, d_gmlp, q_rank, kv_rank, q_scale):
    x = x_ref[0]
    tm = x.shape[0]
    mod = mod_ref[0, 0]
    shift1, scale1 = mod[0:1], mod[1:2]
    h = (_rms(x) * g_mix_ref[...]) * (1.0 + scale1) + shift1
    z = _dot(h.astype(BF16), w_in_ref[...])

    o_q = 2 * d_gmlp
    o_kv = o_q + q_rank
    o_kr = o_kv + kv_rank
    gu = _gelu_tanh(z[:, :d_gmlp])
    gv = _gelu_tanh(z[:, d_gmlp:o_q])
    ln_m = ln_ref[...]

    def group_mean(a):
        hi = a.astype(BF16)
        lo = (a - hi.astype(F32)).astype(BF16)
        return _dot(hi, ln_m) + _dot(lo, ln_m)

    dv = gv - group_mean(gv)
    vn = (dv * lax.rsqrt(group_mean(dv * dv) + EPS)).astype(BF16)

    row = lax.broadcasted_iota(jnp.int32, (CHUNK, CHUNK), 0)
    col = lax.broadcasted_iota(jnp.int32, (CHUNK, CHUNK), 1)
    w_tril = [jnp.where(row >= col, ws_ref[g], 0.0).astype(BF16) for g in range(GMLP_GROUPS)]
    low_half = col < (LANES // 2)
    bias = bias_ref[...]
    mixed_rows = []
    for c in range(tm // CHUNK):
        vc = vn[c * CHUNK:(c + 1) * CHUNK]
        pieces = []
        for j in range(d_gmlp // LANES):
            vp = vc[:, j * LANES:(j + 1) * LANES]
            pieces.append(jnp.where(low_half, _dot(w_tril[2 * j], vp), _dot(w_tril[2 * j + 1], vp)))
        mixed_rows.append(jnp.concatenate(pieces, axis=1) + bias)
    yg = gu * jnp.concatenate(mixed_rows, axis=0)
    yg_ref[0] = (_rms(yg) * g_og_ref[...]).astype(BF16)

    cos_t = cos_ref[0]
    sin_t = sin_ref[0]
    shift = HEAD_SLAB - QK_ROPE_DIM

    c_q = (_rms(z[:, o_q:o_kv]) * g_q_ref[...]).astype(BF16)
    qf = _dot(c_q, w_uq_ref[...])
    for hd in range(MLA_HEADS):
        qs = qf[:, hd * HEAD_SLAB:(hd + 1) * HEAD_SLAB]
        q_ref[0, hd] = ((qs * cos_t + pltpu.roll(qs, shift, 1) * sin_t) * q_scale).astype(BF16)

    c_kv = (_rms(z[:, o_kv:o_kr]) * g_kv_ref[...]).astype(BF16)
    kf = _dot(c_kv, w_uk_ref[...])
    kr = z[:, o_kr:]
    kr = kr * cos_t + pltpu.roll(kr, shift, 1) * sin_t
    for hd in range(MLA_HEADS):
        k_ref[0, hd] = (kf[:, hd * HEAD_SLAB:(hd + 1) * HEAD_SLAB] + kr).astype(BF16)
    vt = lax.dot_general(w_uvt_ref[...], c_kv, (((1,), (1,)), ((), ())),
                         preferred_element_type=F32)
    vt_ref[0] = vt.astype(BF16)


def _mix_in(x, mod, cos_t, sin_t, g_mix, w_in, ws, bias, ln_m, g_q, g_kv, w_uq, w_uk, w_uvt, g_og):
    b, s, d = x.shape
    d_gmlp = g_og.shape[-1]
    q_rank = g_q.shape[-1]
    kv_rank = g_kv.shape[-1]
    tm = TM_IN
    q_scale = (QK_NOPE_DIM + QK_ROPE_DIM) ** -0.5 * math.log2(math.e)

    def full(a):
        return pl.BlockSpec(a.shape, lambda i, j: (0,) * a.ndim)

    kern = functools.partial(_mix_in_kernel, d_gmlp=d_gmlp, q_rank=q_rank,
                             kv_rank=kv_rank, q_scale=q_scale)
    slab = jax.ShapeDtypeStruct((b, MLA_HEADS, s, HEAD_SLAB), BF16)
    return pl.pallas_call(
        kern,
        grid=(b, s // tm),
        in_specs=[
            pl.BlockSpec((1, tm, d), lambda i, j: (i, j, 0)),
            pl.BlockSpec((1, 1, N_MOD, d), lambda i, j: (0, i, 0, 0)),
            pl.BlockSpec((1, tm, HEAD_SLAB), lambda i, j: (i, j, 0)),
            pl.BlockSpec((1, tm, HEAD_SLAB), lambda i, j: (i, j, 0)),
            full(g_mix), full(w_in), full(ws), full(bias), full(ln_m),
            full(g_q), full(g_kv), full(w_uq), full(w_uk), full(w_uvt), full(g_og),
        ],
        out_specs=[
            pl.BlockSpec((1, tm, d_gmlp), lambda i, j: (i, j, 0)),
            pl.BlockSpec((1, MLA_HEADS, tm, HEAD_SLAB), lambda i, j: (i, 0, j, 0)),
            pl.BlockSpec((1, MLA_HEADS, tm, HEAD_SLAB), lambda i, j: (i, 0, j, 0)),
            pl.BlockSpec((1, MLA_HEADS * V_HEAD_DIM, tm), lambda i, j: (i, 0, j)),
        ],
        out_shape=[
            jax.ShapeDtypeStruct((b, s, d_gmlp), BF16),
            slab, slab,
            jax.ShapeDtypeStruct((b, MLA_HEADS * V_HEAD_DIM, s), BF16),
        ],
        compiler_params=pltpu.CompilerParams(
            dimension_semantics=("arbitrary", "arbitrary"),
            vmem_limit_bytes=VMEM_LIMIT_BYTES),
        name="mix_in",
    )(x, mod, cos_t, sin_t, g_mix, w_in, ws, bias, ln_m, g_q, g_kv, w_uq, w_uk, w_uvt, g_og)


def _attn_kernel(q_ref, k_ref, vt_ref, o_ref, *, tq):
    s_len = q_ref.shape[2]
    nt = (((1,), (1,)), ((), ()))
    key = lax.broadcasted_iota(jnp.int32, (tq, tq), 0)
    qry = lax.broadcasted_iota(jnp.int32, (tq, tq), 1)
    causal = key <= qry
    for c in range(s_len // tq):
        q0, kend = c * tq, (c + 1) * tq
        scores = [lax.dot_general(k_ref[0, e, :kend, :], q_ref[0, e, q0:kend, :], nt,
                                  preferred_element_type=F32) for e in range(2)]
        halves = []
        for e in range(2):
            s = scores[e]
            diag = jnp.where(causal, s[q0:], -1e30)
            s = diag if c == 0 else jnp.concatenate([s[:q0], diag], axis=0)
            p = jnp.exp2(s - jnp.max(s, axis=0, keepdims=True))
            l = jnp.sum(p, axis=0, keepdims=True)
            vt = vt_ref[0, e * V_HEAD_DIM:(e + 1) * V_HEAD_DIM, :kend]
            halves.append(_dot(vt, p.astype(BF16)) * (1.0 / l))
        o_ref[0, q0:kend, :] = jnp.concatenate(halves, axis=0).T.astype(o_ref.dtype)


def _attention(q, k, vt):
    b, hds, s, _ = q.shape
    return pl.pallas_call(
        functools.partial(_attn_kernel, tq=TQ),
        grid=(b, hds // 2),
        in_specs=[
            pl.BlockSpec((1, 2, s, HEAD_SLAB), lambda bi, j: (bi, j, 0, 0)),
            pl.BlockSpec((1, 2, s, HEAD_SLAB), lambda bi, j: (bi, j, 0, 0)),
            pl.BlockSpec((1, LANES, s), lambda bi, j: (bi, j, 0)),
        ],
        out_specs=pl.BlockSpec((1, s, LANES), lambda bi, j: (bi, 0, j)),
        out_shape=jax.ShapeDtypeStruct((b, s, hds * V_HEAD_DIM), BF16),
        compiler_params=pltpu.CompilerParams(
            dimension_semantics=("arbitrary", "arbitrary"),
            vmem_limit_bytes=VMEM_LIMIT_BYTES),
        name="attn",
    )(q, k, vt)


def _mix_out_kernel(x_ref, yg_ref, ya_ref, mod_ref, g_oa_ref, w_out_ref, g_ffn_ref,
                    w1_ref, w2_ref, g_fin_ref, o_ref, *, final):
    x = x_ref[0]
    mod = mod_ref[0, 0]
    gate1, shift2, scale2, gate2 = mod[2:3], mod[3:4], mod[4:5], mod[5:6]
    d_g = yg_ref.shape[-1]
    ya = (_rms(ya_ref[0].astype(F32)) * g_oa_ref[...]).astype(BF16)
    mixed = _dot(yg_ref[0], w_out_ref[:d_g, :]) + _dot(ya, w_out_ref[d_g:, :])
    x1 = x + gate1 * mixed

    h = ((_rms(x1) * g_ffn_ref[...]) * (1.0 + scale2) + shift2).astype(BF16)
    f = jnp.zeros_like(x1)
    for c in range(w1_ref.shape[1] // FF_CHUNK):
        a = jnp.maximum(_dot(h, w1_ref[:, c * FF_CHUNK:(c + 1) * FF_CHUNK]), 0.0)
        f = f + _dot((a * a).astype(BF16), w2_ref[c * FF_CHUNK:(c + 1) * FF_CHUNK, :])
    x2 = x1 + gate2 * f
    if final:
        x2 = _rms(x2) * g_fin_ref[...]
    o_ref[0] = x2


def _mix_out(x, yg, ya, mod, g_oa, w_out, g_ffn, w1, w2, g_fin, final):
    b, s, d = x.shape
    tm = TM_OUT

    def full(a):
        return pl.BlockSpec(a.shape, lambda i, j: (0,) * a.ndim,
                            pipeline_mode=pl.Buffered(1))

    def tok(a):
        return pl.BlockSpec((1, tm, a.shape[-1]), lambda i, j: (i, j, 0))

    return pl.pallas_call(
        functools.partial(_mix_out_kernel, final=final),
        grid=(b, s // tm),
        in_specs=[
            tok(x), tok(yg), tok(ya),
            pl.BlockSpec((1, 1, N_MOD, d), lambda i, j: (0, i, 0, 0)),
            full(g_oa), full(w_out), full(g_ffn), full(w1), full(w2), full(g_fin),
        ],
        out_specs=tok(x),
        out_shape=jax.ShapeDtypeStruct(x.shape, F32),
        compiler_params=pltpu.CompilerParams(
            dimension_semantics=("arbitrary", "arbitrary"),
            vmem_limit_bytes=VMEM_LIMIT_BYTES),
        name="mix_out_final" if final else "mix_out",
    )(x, yg, ya, mod, g_oa, w_out, g_ffn, w1, w2, g_fin)


def _rotate_half_cols(w):
    half = w.shape[-1] // 2
    return jnp.concatenate([-w[..., half:], w[..., :half]], axis=-1)


def _prep_weights(w_in, mla_w_uq, mla_w_ukv):
    depth, d, _ = w_in.shape
    q_rank = mla_w_uq.shape[1]
    kv_rank = mla_w_ukv.shape[1]
    o_kr = w_in.shape[-1] - QK_ROPE_DIM
    kr = w_in[..., o_kr:]
    w_in_ext = jnp.concatenate(
        [w_in[..., :o_kr], jnp.zeros((depth, d, QK_NOPE_DIM), F32), kr, _rotate_half_cols(kr)],
        axis=-1).astype(BF16)

    wq = mla_w_uq.reshape(depth, q_rank, MLA_HEADS, QK_NOPE_DIM + QK_ROPE_DIM)
    rope = wq[..., QK_NOPE_DIM:]
    w_uq_ext = jnp.concatenate([wq, _rotate_half_cols(rope)], axis=-1)
    w_uq_ext = w_uq_ext.reshape(depth, q_rank, MLA_HEADS * HEAD_SLAB).astype(BF16)

    wkv = mla_w_ukv.reshape(depth, kv_rank, MLA_HEADS, QK_NOPE_DIM + V_HEAD_DIM)
    w_k = jnp.concatenate(
        [wkv[..., :QK_NOPE_DIM],
         jnp.zeros((depth, kv_rank, MLA_HEADS, HEAD_SLAB - QK_NOPE_DIM), F32)], axis=-1)
    w_uk_ext = w_k.reshape(depth, kv_rank, MLA_HEADS * HEAD_SLAB).astype(BF16)
    w_uvt = jnp.swapaxes(
        wkv[..., QK_NOPE_DIM:].reshape(depth, kv_rank, MLA_HEADS * V_HEAD_DIM), 1, 2).astype(BF16)
    return w_in_ext, w_uq_ext, w_uk_ext, w_uvt


def kernel(x, c, positions, w_ada, b_ada, norm_mix_g, w_in, gmlp_ws, gmlp_bs, mla_q_norm_g,
           mla_kv_norm_g, mla_w_uq, mla_w_ukv, out_norm_gmlp_g, out_norm_mla_g, w_out,
           norm_ffn_g, w_ff1, w_ff2, final_norm_g):
    b, s, d = x.shape
    depth = w_ada.shape[0]
    d_gmlp = out_norm_gmlp_g.shape[-1]
    group_dim = d_gmlp // GMLP_GROUPS

    mod = _adaln(c, w_ada, b_ada).reshape(depth, b, N_MOD, d)
    cos_t, sin_t = _rope_tables(positions)
    w_in_ext, w_uq_ext, w_uk_ext, w_uvt = _prep_weights(w_in, mla_w_uq, mla_w_ukv)
    bias = jnp.repeat(jnp.swapaxes(gmlp_bs, 1, 2), group_dim, axis=-1)
    grp = jnp.arange(d_gmlp) // group_dim
    ln_m = jnp.where(grp[:, None] == grp[None, :], 1.0 / group_dim, 0.0).astype(BF16)
    w_out_b = w_out.astype(BF16)
    w1_b = w_ff1.astype(BF16)
    w2_b = w_ff2.astype(BF16)
    g_fin = final_norm_g.reshape(1, d)

    for l in range(depth):
        yg, q, k, vt = _mix_in(
            x, mod[l:l + 1], cos_t, sin_t, norm_mix_g[l:l + 1], w_in_ext[l], gmlp_ws[l],
            bias[l], ln_m, mla_q_norm_g[l:l + 1], mla_kv_norm_g[l:l + 1], w_uq_ext[l],
            w_uk_ext[l], w_uvt[l], out_norm_gmlp_g[l:l + 1])
        ya = _attention(q, k, vt)
        x = _mix_out(x, yg, ya, mod[l:l + 1], out_norm_mla_g[l:l + 1], w_out_b[l],
                     norm_ffn_g[l:l + 1], w1_b[l], w2_b[l], g_fin, final=(l == depth - 1))
    return x
```

```python
import functools
import math

import jax
import jax.numpy as jnp
from jax import lax
from jax.experimental import pallas as pl
from jax.experimental.pallas import tpu as pltpu

F32 = jnp.float32
BF16 = jnp.bfloat16

EPS = 1e-6
ROPE_THETA = 10000.0
N_MOD = 6

GMLP_GROUPS = 8
CHUNK = 128
MLA_HEADS = 8
QK_NOPE_DIM = 64
QK_ROPE_DIM = 32
V_HEAD_DIM = 64
HEAD_SLAB = 128
LANES = 128
BF16_ROWS = 16
V_SLAB = V_HEAD_DIM + BF16_ROWS

VMEM_LIMIT_BYTES = 56 * 1024 * 1024

TM_IN = 512
TM_OUT = 512
TQ = 256
FF_CHUNK = 1024
ADA_TN = 1536


def _dot(a, b):
    return jnp.dot(a, b, preferred_element_type=F32)


def _rms(x):
    return x * lax.rsqrt(jnp.mean(x * x, axis=-1, keepdims=True) + EPS)


def _gelu_tanh(x):
    c = math.sqrt(2.0 / math.pi)
    half = 0.5 * x
    return half + half * jnp.tanh(x * (c + (c * 0.044715) * (x * x)))


def _adaln_kernel(c_ref, w_ref, b_ref, o_ref):
    c = c_ref[...]
    c_act = (c / (1.0 + jnp.exp(-c))).astype(BF16)
    o_ref[0] = _dot(c_act, w_ref[0].astype(BF16)) + b_ref[0]


def _adaln(c, w_ada, b_ada):
    depth, d, n = w_ada.shape
    b = c.shape[0]
    return pl.pallas_call(
        _adaln_kernel,
        grid=(depth, n // ADA_TN),
        in_specs=[
            pl.BlockSpec((b, d), lambda l, j: (0, 0)),
            pl.BlockSpec((1, d, ADA_TN), lambda l, j: (l, 0, j)),
            pl.BlockSpec((1, 1, ADA_TN), lambda l, j: (l, 0, j)),
        ],
        out_specs=pl.BlockSpec((1, b, ADA_TN), lambda l, j: (l, 0, j)),
        out_shape=jax.ShapeDtypeStruct((depth, b, n), F32),
        compiler_params=pltpu.CompilerParams(
            dimension_semantics=("arbitrary", "arbitrary"),
            vmem_limit_bytes=VMEM_LIMIT_BYTES),
        name="adaln_mod",
    )(c, w_ada, b_ada.reshape(depth, 1, n))


def _rope_kernel(pos_ref, freq_ref, cos_ref, sin_ref):
    pos = pos_ref[0].astype(F32)
    ang = pos * freq_ref[...]
    lane = lax.broadcasted_iota(jnp.int32, ang.shape, 1)
    rope = (lane >= QK_NOPE_DIM) & (lane < QK_NOPE_DIM + QK_ROPE_DIM)
    cos_ref[0] = jnp.where(rope, jnp.cos(ang), jnp.where(lane < QK_NOPE_DIM, 1.0, 0.0))
    sin_ref[0] = jnp.where(rope, jnp.sin(ang), 0.0)


def _rope_tables(positions):
    b, s = positions.shape
    half = QK_ROPE_DIM // 2
    freqs = ROPE_THETA ** (-jnp.arange(0, QK_ROPE_DIM, 2, dtype=F32) / QK_ROPE_DIM)
    freq_lane = jnp.concatenate(
        [jnp.zeros((QK_NOPE_DIM,), F32), freqs, freqs,
         jnp.zeros((HEAD_SLAB - QK_NOPE_DIM - 2 * half,), F32)]).reshape(1, HEAD_SLAB)
    ts = 512
    out = jax.ShapeDtypeStruct((b, s, HEAD_SLAB), F32)
    return pl.pallas_call(
        _rope_kernel,
        grid=(b, s // ts),
        in_specs=[
            pl.BlockSpec((1, ts, 1), lambda i, j: (i, j, 0)),
            pl.BlockSpec((1, HEAD_SLAB), lambda i, j: (0, 0)),
        ],
        out_specs=[pl.BlockSpec((1, ts, HEAD_SLAB), lambda i, j: (i, j, 0))] * 2,
        out_shape=[out, out],
        compiler_params=pltpu.CompilerParams(
            dimension_semantics=("arbitrary", "arbitrary")),
        name="rope_tables",
    )(positions.reshape(b, s, 1), freq_lane)


def _mix_in_kernel(x_ref, mod_ref, cos_ref, sin_ref, g_mix_ref, w_uvt_in_ref, w_in_ref, wst_ref,
                   bias_ref, g_q_ref, g_kv_ref, w_uq_ref, w_uk_ref, w_uvt_ref, v_one_ref,
                   g_og_ref, yg_ref, q_ref, k_ref, vt_ref, *, d_gmlp, q_rank, kv_rank, q_scale):
    x = x_ref[0]
    tm = x.shape[0]
    mod = mod_ref[0, 0]
    shift1, scale1 = mod[0:1], mod[1:2]
    h = (_rms(x) * (g_mix_ref[...] * (1.0 + scale1)) + shift1).astype(BF16)
    nt = (((1,), (1,)), ((), ()))
    z = _dot(h, w_in_ref[...])
    uv_t = lax.dot_general(w_uvt_in_ref[...], h, nt, preferred_element_type=F32)

    n_chunk = tm // CHUNK
    dg = d_gmlp // GMLP_GROUPS
    gu = _gelu_tanh(uv_t[:d_gmlp]).reshape(GMLP_GROUPS, dg, tm)
    gv = _gelu_tanh(uv_t[d_gmlp:]).reshape(GMLP_GROUPS, dg, tm)
    dv = gv - jnp.mean(gv, axis=1, keepdims=True)
    vn = (dv * lax.rsqrt(jnp.mean(dv * dv, axis=1, keepdims=True) + EPS)).astype(BF16)

    row = lax.broadcasted_iota(jnp.int32, (CHUNK, CHUNK), 0)
    col = lax.broadcasted_iota(jnp.int32, (CHUNK, CHUNK), 1)
    mixed = []
    for g in range(GMLP_GROUPS):
        w_g = jnp.where(row <= col, wst_ref[g], 0.0).astype(BF16)
        lhs = jnp.concatenate([vn[g, :, c * CHUNK:(c + 1) * CHUNK] for c in range(n_chunk)], axis=0)
        out = _dot(lhs, w_g)
        out = jnp.concatenate([out[c * dg:(c + 1) * dg] for c in range(n_chunk)], axis=1)
        mixed.append(out + jnp.concatenate([bias_ref[g]] * n_chunk, axis=1))
    yg_t = (gu * jnp.stack(mixed)).reshape(d_gmlp, tm)
    yg_t = yg_t * lax.rsqrt(jnp.mean(yg_t * yg_t, axis=0, keepdims=True) + EPS) * g_og_ref[...]
    yg_ref[0] = yg_t.T.astype(BF16)

    cos_t = cos_ref[0]
    sin_t = sin_ref[0]
    shift = HEAD_SLAB - QK_ROPE_DIM

    o_kv = q_rank
    o_kr = o_kv + kv_rank
    c_q = (_rms(z[:, :o_kv]) * g_q_ref[...]).astype(BF16)
    qf = _dot(c_q, w_uq_ref[...])
    cos_q = cos_t * q_scale
    sin_q = sin_t * q_scale
    for hd in range(MLA_HEADS):
        qs = qf[:, hd * HEAD_SLAB:(hd + 1) * HEAD_SLAB]
        q_ref[0, hd] = (qs * cos_q + pltpu.roll(qs, shift, 1) * sin_q).astype(BF16)

    c_kv = (_rms(z[:, o_kv:o_kr]) * g_kv_ref[...]).astype(BF16)
    kf = _dot(c_kv, w_uk_ref[...])
    kr = z[:, o_kr:]
    kr = kr * cos_t + pltpu.roll(kr, shift, 1) * sin_t
    for hd in range(MLA_HEADS):
        k_ref[0, hd] = (kf[:, hd * HEAD_SLAB:(hd + 1) * HEAD_SLAB] + kr).astype(BF16)
    vt = lax.dot_general(w_uvt_ref[...], c_kv, nt,
                         preferred_element_type=F32)
    vt_ref[0] = (vt + v_one_ref[...]).astype(BF16)


def _mix_in(x, mod, cos_t, sin_t, g_mix, w_uvt_in, w_in, wst, bias, g_q, g_kv, w_uq, w_uk, w_uvt,
            v_one, g_og):
    b, s, d = x.shape
    d_gmlp = g_og.shape[0]
    q_rank = g_q.shape[-1]
    kv_rank = g_kv.shape[-1]
    tm = TM_IN
    q_scale = (QK_NOPE_DIM + QK_ROPE_DIM) ** -0.5 * math.log2(math.e)

    def full(a):
        return pl.BlockSpec(a.shape, lambda i, j: (0,) * a.ndim)

    kern = functools.partial(_mix_in_kernel, d_gmlp=d_gmlp, q_rank=q_rank,
                             kv_rank=kv_rank, q_scale=q_scale)
    slab = jax.ShapeDtypeStruct((b, MLA_HEADS, s, HEAD_SLAB), BF16)
    return pl.pallas_call(
        kern,
        grid=(b, s // tm),
        in_specs=[
            pl.BlockSpec((1, tm, d), lambda i, j: (i, j, 0)),
            pl.BlockSpec((1, 1, N_MOD, d), lambda i, j: (0, i, 0, 0)),
            pl.BlockSpec((1, tm, HEAD_SLAB), lambda i, j: (i, j, 0)),
            pl.BlockSpec((1, tm, HEAD_SLAB), lambda i, j: (i, j, 0)),
            full(g_mix), full(w_uvt_in), full(w_in), full(wst), full(bias),
            full(g_q), full(g_kv), full(w_uq), full(w_uk), full(w_uvt), full(v_one), full(g_og),
        ],
        out_specs=[
            pl.BlockSpec((1, tm, d_gmlp), lambda i, j: (i, j, 0)),
            pl.BlockSpec((1, MLA_HEADS, tm, HEAD_SLAB), lambda i, j: (i, 0, j, 0)),
            pl.BlockSpec((1, MLA_HEADS, tm, HEAD_SLAB), lambda i, j: (i, 0, j, 0)),
            pl.BlockSpec((1, MLA_HEADS * V_SLAB, tm), lambda i, j: (i, 0, j)),
        ],
        out_shape=[
            jax.ShapeDtypeStruct((b, s, d_gmlp), BF16),
            slab, slab,
            jax.ShapeDtypeStruct((b, MLA_HEADS * V_SLAB, s), BF16),
        ],
        compiler_params=pltpu.CompilerParams(
            dimension_semantics=("arbitrary", "arbitrary"),
            vmem_limit_bytes=VMEM_LIMIT_BYTES),
        name="mix_in",
    )(x, mod, cos_t, sin_t, g_mix, w_uvt_in, w_in, wst, bias, g_q, g_kv, w_uq, w_uk, w_uvt, v_one,
      g_og)


def _attn_kernel(q_ref, k_ref, vt_ref, o_ref, *, tq):
    s_len = q_ref.shape[2]
    nt = (((1,), (1,)), ((), ()))
    key = lax.broadcasted_iota(jnp.int32, (tq, tq), 0)
    qry = lax.broadcasted_iota(jnp.int32, (tq, tq), 1)
    causal = key <= qry
    n_blk = s_len // tq

    def qk(c):
        q0, kend = c * tq, (c + 1) * tq
        return [lax.dot_general(k_ref[0, e, :kend, :], q_ref[0, e, q0:kend, :], nt,
                                preferred_element_type=F32) for e in range(2)]

    nxt = qk(0)
    for c in range(n_blk):
        q0, kend = c * tq, (c + 1) * tq
        scores = nxt
        if c + 1 < n_blk:
            nxt = qk(c + 1)
        halves = []
        for e in range(2):
            s = scores[e]
            diag = jnp.where(causal, s[q0:], -1e30)
            s = diag if c == 0 else jnp.concatenate([s[:q0], diag], axis=0)
            p = jnp.exp2(s - jnp.max(s, axis=0, keepdims=True)).astype(BF16)
            o = _dot(vt_ref[0, e * V_SLAB:(e + 1) * V_SLAB, :kend], p)
            halves.append(o[:V_HEAD_DIM] * (1.0 / o[V_HEAD_DIM:V_HEAD_DIM + 1]))
        o_ref[0, q0:kend, :] = jnp.concatenate(halves, axis=0).T.astype(o_ref.dtype)


def _attention(q, k, vt):
    b, hds, s, _ = q.shape
    return pl.pallas_call(
        functools.partial(_attn_kernel, tq=TQ),
        grid=(b, hds // 2),
        in_specs=[
            pl.BlockSpec((1, 2, s, HEAD_SLAB), lambda bi, j: (bi, j, 0, 0)),
            pl.BlockSpec((1, 2, s, HEAD_SLAB), lambda bi, j: (bi, j, 0, 0)),
            pl.BlockSpec((1, 2 * V_SLAB, s), lambda bi, j: (bi, j, 0)),
        ],
        out_specs=pl.BlockSpec((1, s, LANES), lambda bi, j: (bi, 0, j)),
        out_shape=jax.ShapeDtypeStruct((b, s, hds * V_HEAD_DIM), BF16),
        compiler_params=pltpu.CompilerParams(
            dimension_semantics=("arbitrary", "arbitrary"),
            vmem_limit_bytes=VMEM_LIMIT_BYTES),
        name="attn",
    )(q, k, vt)


def _mix_out_kernel(x_ref, yg_ref, ya_ref, mod_ref, g_oa_ref, w_out_ref, g_ffn_ref,
                    w1_ref, w2_ref, g_fin_ref, o_ref, *, final):
    x = x_ref[0]
    mod = mod_ref[0, 0]
    gate1, shift2, scale2, gate2 = mod[2:3], mod[3:4], mod[4:5], mod[5:6]
    d_g = yg_ref.shape[-1]
    ya = (_rms(ya_ref[0].astype(F32)) * g_oa_ref[...]).astype(BF16)
    mixed = _dot(yg_ref[0], w_out_ref[:d_g, :]) + _dot(ya, w_out_ref[d_g:, :])
    x1 = x + gate1 * mixed

    h = (_rms(x1) * (g_ffn_ref[...] * (1.0 + scale2)) + shift2).astype(BF16)
    f = jnp.zeros_like(x1)
    for c in range(w1_ref.shape[1] // FF_CHUNK):
        a = jnp.maximum(_dot(h, w1_ref[:, c * FF_CHUNK:(c + 1) * FF_CHUNK]), 0.0)
        f = f + _dot((a * a).astype(BF16), w2_ref[c * FF_CHUNK:(c + 1) * FF_CHUNK, :])
    x2 = x1 + gate2 * f
    if final:
        x2 = _rms(x2) * g_fin_ref[...]
    o_ref[0] = x2


def _mix_out(x, yg, ya, mod, g_oa, w_out, g_ffn, w1, w2, g_fin, final):
    b, s, d = x.shape
    tm = TM_OUT

    def full(a):
        return pl.BlockSpec(a.shape, lambda i, j: (0,) * a.ndim,
                            pipeline_mode=pl.Buffered(1))

    def tok(a):
        return pl.BlockSpec((1, tm, a.shape[-1]), lambda i, j: (i, j, 0))

    return pl.pallas_call(
        functools.partial(_mix_out_kernel, final=final),
        grid=(b, s // tm),
        in_specs=[
            tok(x), tok(yg), tok(ya),
            pl.BlockSpec((1, 1, N_MOD, d), lambda i, j: (0, i, 0, 0)),
            full(g_oa), full(w_out), full(g_ffn), full(w1), full(w2), full(g_fin),
        ],
        out_specs=tok(x),
        out_shape=jax.ShapeDtypeStruct(x.shape, F32),
        compiler_params=pltpu.CompilerParams(
            dimension_semantics=("arbitrary", "arbitrary"),
            vmem_limit_bytes=VMEM_LIMIT_BYTES),
        name="mix_out_final" if final else "mix_out",
    )(x, yg, ya, mod, g_oa, w_out, g_ffn, w1, w2, g_fin)


def _rotate_half_cols(w):
    half = w.shape[-1] // 2
    return jnp.concatenate([-w[..., half:], w[..., :half]], axis=-1)


def _prep_weights(w_in, mla_w_uq, mla_w_ukv, d_gmlp):
    depth, d, _ = w_in.shape
    q_rank = mla_w_uq.shape[1]
    kv_rank = mla_w_ukv.shape[1]
    o_kr = w_in.shape[-1] - QK_ROPE_DIM
    kr = w_in[..., o_kr:]
    w_uvt_in = jnp.swapaxes(w_in[..., :2 * d_gmlp], 1, 2).astype(BF16)
    w_in_ext = jnp.concatenate(
        [w_in[..., 2 * d_gmlp:o_kr], jnp.zeros((depth, d, QK_NOPE_DIM), F32), kr,
         _rotate_half_cols(kr)], axis=-1).astype(BF16)

    wq = mla_w_uq.reshape(depth, q_rank, MLA_HEADS, QK_NOPE_DIM + QK_ROPE_DIM)
    rope = wq[..., QK_NOPE_DIM:]
    w_uq_ext = jnp.concatenate([wq, _rotate_half_cols(rope)], axis=-1)
    w_uq_ext = w_uq_ext.reshape(depth, q_rank, MLA_HEADS * HEAD_SLAB).astype(BF16)

    wkv = mla_w_ukv.reshape(depth, kv_rank, MLA_HEADS, QK_NOPE_DIM + V_HEAD_DIM)
    w_k = jnp.concatenate(
        [wkv[..., :QK_NOPE_DIM],
         jnp.zeros((depth, kv_rank, MLA_HEADS, HEAD_SLAB - QK_NOPE_DIM), F32)], axis=-1)
    w_uk_ext = w_k.reshape(depth, kv_rank, MLA_HEADS * HEAD_SLAB).astype(BF16)
    w_v = jnp.concatenate(
        [wkv[..., QK_NOPE_DIM:], jnp.zeros((depth, kv_rank, MLA_HEADS, BF16_ROWS), F32)], axis=-1)
    w_uvt = jnp.swapaxes(w_v.reshape(depth, kv_rank, MLA_HEADS * V_SLAB), 1, 2).astype(BF16)
    v_one = (jnp.arange(MLA_HEADS * V_SLAB) % V_SLAB == V_HEAD_DIM).astype(F32).reshape(-1, 1)
    return w_uvt_in, w_in_ext, w_uq_ext, w_uk_ext, w_uvt, v_one


def kernel(x, c, positions, w_ada, b_ada, norm_mix_g, w_in, gmlp_ws, gmlp_bs, mla_q_norm_g,
           mla_kv_norm_g, mla_w_uq, mla_w_ukv, out_norm_gmlp_g, out_norm_mla_g, w_out,
           norm_ffn_g, w_ff1, w_ff2, final_norm_g):
    b, s, d = x.shape
    depth = w_ada.shape[0]
    d_gmlp = out_norm_gmlp_g.shape[-1]

    mod = _adaln(c, w_ada, b_ada).reshape(depth, b, N_MOD, d)
    cos_t, sin_t = _rope_tables(positions)
    w_uvt_in, w_in_ext, w_uq_ext, w_uk_ext, w_uvt, v_one = _prep_weights(
        w_in, mla_w_uq, mla_w_ukv, d_gmlp)
    wst = jnp.swapaxes(gmlp_ws, 2, 3)
    bias = gmlp_bs[:, :, None, :]
    g_og = out_norm_gmlp_g[:, :, None]
    w_out_b = w_out.astype(BF16)
    w1_b = w_ff1.astype(BF16)
    w2_b = w_ff2.astype(BF16)
    g_fin = final_norm_g.reshape(1, d)

    for l in range(depth):
        yg, q, k, vt = _mix_in(
            x, mod[l:l + 1], cos_t, sin_t, norm_mix_g[l:l + 1], w_uvt_in[l], w_in_ext[l], wst[l],
            bias[l], mla_q_norm_g[l:l + 1], mla_kv_norm_g[l:l + 1], w_uq_ext[l],
            w_uk_ext[l], w_uvt[l], v_one, g_og[l])
        ya = _attention(q, k, vt)
        x = _mix_out(x, yg, ya, mod[l:l + 1], out_norm_mla_g[l:l + 1], w_out_b[l],
                     norm_ffn_g[l:l + 1], w1_b[l], w2_b[l], g_fin, final=(l == depth - 1))
    return x
```

```python
import functools
import math

import jax
import jax.numpy as jnp
from jax import lax
from jax.experimental import pallas as pl
from jax.experimental.pallas import tpu as pltpu

F32 = jnp.float32
BF16 = jnp.bfloat16

EPS = 1e-6
ROPE_THETA = 10000.0
N_MOD = 6

GMLP_GROUPS = 8
CHUNK = 128
MLA_HEADS = 8
QK_NOPE_DIM = 64
QK_ROPE_DIM = 32
V_HEAD_DIM = 64
HEAD_SLAB = 128
LANES = 128
BF16_ROWS = 16
V_SLAB = V_HEAD_DIM + BF16_ROWS

VMEM_LIMIT_BYTES = 56 * 1024 * 1024

TM_IN = 1024
TM_SUB = 512
TM_OUT = 1024
TQ = 256
FF_CHUNK = 1024
ADA_TN = 1536


def _dot(a, b):
    return jnp.dot(a, b, preferred_element_type=F32)


def _rms(x):
    return x * lax.rsqrt(jnp.mean(x * x, axis=-1, keepdims=True) + EPS)


def _gelu_tanh(x):
    c = math.sqrt(2.0 / math.pi)
    half = 0.5 * x
    return half + half * jnp.tanh(x * (c + (c * 0.044715) * (x * x)))


def _adaln_kernel(c_ref, w_ref, b_ref, o_ref):
    c = c_ref[...]
    c_act = (c / (1.0 + jnp.exp(-c))).astype(BF16)
    o_ref[0] = _dot(c_act, w_ref[0].astype(BF16)) + b_ref[0]


def _adaln(c, w_ada, b_ada):
    depth, d, n = w_ada.shape
    b = c.shape[0]
    return pl.pallas_call(
        _adaln_kernel,
        grid=(depth, n // ADA_TN),
        in_specs=[
            pl.BlockSpec((b, d), lambda l, j: (0, 0)),
            pl.BlockSpec((1, d, ADA_TN), lambda l, j: (l, 0, j)),
            pl.BlockSpec((1, 1, ADA_TN), lambda l, j: (l, 0, j)),
        ],
        out_specs=pl.BlockSpec((1, b, ADA_TN), lambda l, j: (l, 0, j)),
        out_shape=jax.ShapeDtypeStruct((depth, b, n), F32),
        compiler_params=pltpu.CompilerParams(
            dimension_semantics=("arbitrary", "arbitrary"),
            vmem_limit_bytes=VMEM_LIMIT_BYTES),
        name="adaln_mod",
    )(c, w_ada, b_ada.reshape(depth, 1, n))


def _rope_kernel(pos_ref, freq_ref, cos_ref, sin_ref):
    pos = pos_ref[0].astype(F32)
    lane = lax.broadcasted_iota(jnp.int32, (pos.shape[0], HEAD_SLAB), 1)
    rope = (lane >= QK_NOPE_DIM) & (lane < QK_NOPE_DIM + QK_ROPE_DIM)
    ang = pos * freq_ref[...] - jnp.where(lane >= QK_NOPE_DIM + QK_ROPE_DIM, 0.5 * math.pi, 0.0)
    c = jnp.cos(ang)
    cos_ref[0] = jnp.where(rope, c, jnp.where(lane < QK_NOPE_DIM, 1.0, 0.0))
    sin_ref[0] = jnp.where(rope, pltpu.roll(c, HEAD_SLAB - QK_ROPE_DIM, 1), 0.0)


def _rope_tables(positions):
    b, s = positions.shape
    freqs = ROPE_THETA ** (-jnp.arange(0, QK_ROPE_DIM, 2, dtype=F32) / QK_ROPE_DIM)
    freq_lane = jnp.concatenate(
        [jnp.zeros((QK_NOPE_DIM,), F32), freqs, freqs, freqs, freqs]).reshape(1, HEAD_SLAB)
    ts = 512
    out = jax.ShapeDtypeStruct((b, s, HEAD_SLAB), F32)
    return pl.pallas_call(
        _rope_kernel,
        grid=(b, s // ts),
        in_specs=[
            pl.BlockSpec((1, ts, 1), lambda i, j: (i, j, 0)),
            pl.BlockSpec((1, HEAD_SLAB), lambda i, j: (0, 0)),
        ],
        out_specs=[pl.BlockSpec((1, ts, HEAD_SLAB), lambda i, j: (i, j, 0))] * 2,
        out_shape=[out, out],
        compiler_params=pltpu.CompilerParams(
            dimension_semantics=("arbitrary", "arbitrary")),
        name="rope_tables",
    )(positions.reshape(b, s, 1), freq_lane)


def _mix_in_kernel(x_ref, mod_ref, cos_ref, sin_ref, g_mix_ref, w_uvt_in_ref, w_in_ref, wst_ref,
                   bias_ref, g_q_ref, g_kv_ref, w_uq_ref, w_uk_ref, w_uvt_ref, v_one_ref,
                   g_og_ref, yg_ref, q_ref, k_ref, vt_ref, *, d_gmlp, q_rank, kv_rank, q_scale, tm_sub):
    nt = (((1,), (1,)), ((), ()))
    mod = mod_ref[0, 0]
    shift1, scale1 = mod[0:1], mod[1:2]
    gain1 = g_mix_ref[...] * (1.0 + scale1)
    n_chunk = tm_sub // CHUNK
    dg = d_gmlp // GMLP_GROUPS
    o_kv = q_rank
    o_kr = o_kv + kv_rank
    shift = HEAD_SLAB - QK_ROPE_DIM

    row = lax.broadcasted_iota(jnp.int32, (CHUNK, CHUNK), 0)
    col = lax.broadcasted_iota(jnp.int32, (CHUNK, CHUNK), 1)
    w_mix = [jnp.where(row <= col, wst_ref[g], 0.0).astype(BF16) for g in range(GMLP_GROUPS)]
    bias = [jnp.concatenate([bias_ref[g]] * n_chunk, axis=1) for g in range(GMLP_GROUPS)]

    n_sub = x_ref.shape[1] // tm_sub
    proj = []
    for sb in range(n_sub):
        rows = slice(sb * tm_sub, (sb + 1) * tm_sub)
        h = (_rms(x_ref[0, rows, :]) * gain1 + shift1).astype(BF16)
        proj.append((_dot(h, w_in_ref[...]),
                     lax.dot_general(w_uvt_in_ref[...], h, nt, preferred_element_type=F32)))

    for sb in range(n_sub):
        rows = slice(sb * tm_sub, (sb + 1) * tm_sub)
        z, uv_t = proj[sb]

        gu = _gelu_tanh(uv_t[:d_gmlp]).reshape(GMLP_GROUPS, dg, tm_sub)
        gv = _gelu_tanh(uv_t[d_gmlp:]).reshape(GMLP_GROUPS, dg, tm_sub)
        dv = gv - jnp.mean(gv, axis=1, keepdims=True)
        vn = (dv * lax.rsqrt(jnp.mean(dv * dv, axis=1, keepdims=True) + EPS)).astype(BF16)
        mixed = []
        for g in range(GMLP_GROUPS):
            lhs = jnp.concatenate([vn[g, :, c * CHUNK:(c + 1) * CHUNK] for c in range(n_chunk)], axis=0)
            out = _dot(lhs, w_mix[g])
            out = jnp.concatenate([out[c * dg:(c + 1) * dg] for c in range(n_chunk)], axis=1)
            mixed.append(out + bias[g])
        yg_t = (gu * jnp.stack(mixed)).reshape(d_gmlp, tm_sub)
        yg_t = yg_t * lax.rsqrt(jnp.mean(yg_t * yg_t, axis=0, keepdims=True) + EPS) * g_og_ref[...]
        yg_ref[0, rows, :] = yg_t.T.astype(BF16)

        cos_t = cos_ref[0, rows, :]
        sin_t = sin_ref[0, rows, :]
        c_q = (_rms(z[:, :o_kv]) * g_q_ref[...]).astype(BF16)
        qf = _dot(c_q, w_uq_ref[...])
        cos_q = cos_t * q_scale
        sin_q = sin_t * q_scale
        for hd in range(MLA_HEADS):
            qs = qf[:, hd * HEAD_SLAB:(hd + 1) * HEAD_SLAB]
            q_ref[0, hd, rows, :] = (qs * cos_q + pltpu.roll(qs, shift, 1) * sin_q).astype(BF16)

        c_kv = (_rms(z[:, o_kv:o_kr]) * g_kv_ref[...]).astype(BF16)
        kf = _dot(c_kv, w_uk_ref[...])
        kr = z[:, o_kr:]
        kr = kr * cos_t + pltpu.roll(kr, shift, 1) * sin_t
        for hd in range(MLA_HEADS):
            k_ref[0, hd, rows, :] = (kf[:, hd * HEAD_SLAB:(hd + 1) * HEAD_SLAB] + kr).astype(BF16)
        vt = lax.dot_general(w_uvt_ref[...], c_kv, nt,
                             preferred_element_type=F32)
        vt_ref[0, :, rows] = (vt + v_one_ref[...]).astype(BF16)


def _mix_in(x, mod, cos_t, sin_t, g_mix, w_uvt_in, w_in, wst, bias, g_q, g_kv, w_uq, w_uk, w_uvt,
            v_one, g_og):
    b, s, d = x.shape
    d_gmlp = g_og.shape[0]
    q_rank = g_q.shape[-1]
    kv_rank = g_kv.shape[-1]
    tm = TM_IN
    q_scale = (QK_NOPE_DIM + QK_ROPE_DIM) ** -0.5 * math.log2(math.e)

    def full(a):
        return pl.BlockSpec(a.shape, lambda i, j: (0,) * a.ndim)

    kern = functools.partial(_mix_in_kernel, d_gmlp=d_gmlp, q_rank=q_rank,
                             kv_rank=kv_rank, q_scale=q_scale, tm_sub=TM_SUB)
    slab = jax.ShapeDtypeStruct((b, MLA_HEADS, s, HEAD_SLAB), BF16)
    return pl.pallas_call(
        kern,
        grid=(b, s // tm),
        in_specs=[
            pl.BlockSpec((1, tm, d), lambda i, j: (i, j, 0)),
            pl.BlockSpec((1, 1, N_MOD, d), lambda i, j: (0, i, 0, 0)),
            pl.BlockSpec((1, tm, HEAD_SLAB), lambda i, j: (i, j, 0)),
            pl.BlockSpec((1, tm, HEAD_SLAB), lambda i, j: (i, j, 0)),
            full(g_mix), full(w_uvt_in), full(w_in), full(wst), full(bias),
            full(g_q), full(g_kv), full(w_uq), full(w_uk), full(w_uvt), full(v_one), full(g_og),
        ],
        out_specs=[
            pl.BlockSpec((1, tm, d_gmlp), lambda i, j: (i, j, 0)),
            pl.BlockSpec((1, MLA_HEADS, tm, HEAD_SLAB), lambda i, j: (i, 0, j, 0)),
            pl.BlockSpec((1, MLA_HEADS, tm, HEAD_SLAB), lambda i, j: (i, 0, j, 0)),
            pl.BlockSpec((1, MLA_HEADS * V_SLAB, tm), lambda i, j: (i, 0, j)),
        ],
        out_shape=[
            jax.ShapeDtypeStruct((b, s, d_gmlp), BF16),
            slab, slab,
            jax.ShapeDtypeStruct((b, MLA_HEADS * V_SLAB, s), BF16),
        ],
        compiler_params=pltpu.CompilerParams(
            dimension_semantics=("arbitrary", "arbitrary"),
            vmem_limit_bytes=VMEM_LIMIT_BYTES),
        name="mix_in",
    )(x, mod, cos_t, sin_t, g_mix, w_uvt_in, w_in, wst, bias, g_q, g_kv, w_uq, w_uk, w_uvt, v_one,
      g_og)


def _attn_kernel(q_ref, k_ref, vt_ref, o_ref, *, tq):
    s_len = q_ref.shape[2]
    nt = (((1,), (1,)), ((), ()))
    key = lax.broadcasted_iota(jnp.int32, (tq, tq), 0)
    qry = lax.broadcasted_iota(jnp.int32, (tq, tq), 1)
    causal = key <= qry
    n_blk = s_len // tq

    def qk(c):
        q0, kend = c * tq, (c + 1) * tq
        return [lax.dot_general(k_ref[0, e, :kend, :], q_ref[0, e, q0:kend, :], nt,
                                preferred_element_type=F32) for e in range(2)]

    nxt = qk(0)
    for c in range(n_blk):
        q0, kend = c * tq, (c + 1) * tq
        scores = nxt
        if c + 1 < n_blk:
            nxt = qk(c + 1)
        halves = []
        for e in range(2):
            s = scores[e]
            diag = jnp.where(causal, s[q0:], -1e30)
            s = diag if c == 0 else jnp.concatenate([s[:q0], diag], axis=0)
            p = jnp.exp2(s - jnp.max(s, axis=0, keepdims=True)).astype(BF16)
            o = _dot(vt_ref[0, e * V_SLAB:(e + 1) * V_SLAB, :kend], p)
            halves.append(o[:V_HEAD_DIM] * (1.0 / o[V_HEAD_DIM:V_HEAD_DIM + 1]))
        o_ref[0, q0:kend, :] = jnp.concatenate(halves, axis=0).T.astype(o_ref.dtype)


def _attention(q, k, vt):
    b, hds, s, _ = q.shape
    return pl.pallas_call(
        functools.partial(_attn_kernel, tq=TQ),
        grid=(b, hds // 2),
        in_specs=[
            pl.BlockSpec((1, 2, s, HEAD_SLAB), lambda bi, j: (bi, j, 0, 0)),
            pl.BlockSpec((1, 2, s, HEAD_SLAB), lambda bi, j: (bi, j, 0, 0)),
            pl.BlockSpec((1, 2 * V_SLAB, s), lambda bi, j: (bi, j, 0)),
        ],
        out_specs=pl.BlockSpec((1, s, LANES), lambda bi, j: (bi, 0, j)),
        out_shape=jax.ShapeDtypeStruct((b, s, hds * V_HEAD_DIM), BF16),
        compiler_params=pltpu.CompilerParams(
            dimension_semantics=("arbitrary", "arbitrary"),
            vmem_limit_bytes=VMEM_LIMIT_BYTES),
        name="attn",
    )(q, k, vt)


def _mix_out_kernel(x_ref, yg_ref, ya_ref, mod_ref, g_oa_ref, w_out_ref, g_ffn_ref,
                    w1_ref, w2_ref, g_fin_ref, o_ref, *, final, tm_sub):
    mod = mod_ref[0, 0]
    gate1, shift2, scale2, gate2 = mod[2:3], mod[3:4], mod[4:5], mod[5:6]
    gain2 = g_ffn_ref[...] * (1.0 + scale2)
    d_g = yg_ref.shape[-1]
    n_sub = x_ref.shape[1] // tm_sub

    mids = []
    for sb in range(n_sub):
        rows = slice(sb * tm_sub, (sb + 1) * tm_sub)
        ya = (_rms(ya_ref[0, rows, :].astype(F32)) * g_oa_ref[...]).astype(BF16)
        mixed = _dot(yg_ref[0, rows, :], w_out_ref[:d_g, :]) + _dot(ya, w_out_ref[d_g:, :])
        x1 = x_ref[0, rows, :] + gate1 * mixed
        mids.append((x1, (_rms(x1) * gain2 + shift2).astype(BF16)))

    for sb in range(n_sub):
        rows = slice(sb * tm_sub, (sb + 1) * tm_sub)
        x1, h = mids[sb]
        f = jnp.zeros_like(x1)
        for c in range(w1_ref.shape[1] // FF_CHUNK):
            a = jnp.maximum(_dot(h, w1_ref[:, c * FF_CHUNK:(c + 1) * FF_CHUNK]), 0.0)
            f = f + _dot((a * a).astype(BF16), w2_ref[c * FF_CHUNK:(c + 1) * FF_CHUNK, :])
        x2 = x1 + gate2 * f
        if final:
            x2 = _rms(x2) * g_fin_ref[...]
        o_ref[0, rows, :] = x2


def _mix_out(x, yg, ya, mod, g_oa, w_out, g_ffn, w1, w2, g_fin, final):
    b, s, d = x.shape
    tm = TM_OUT

    def full(a):
        return pl.BlockSpec(a.shape, lambda i, j: (0,) * a.ndim,
                            pipeline_mode=pl.Buffered(1))

    def tok(a):
        return pl.BlockSpec((1, tm, a.shape[-1]), lambda i, j: (i, j, 0))

    return pl.pallas_call(
        functools.partial(_mix_out_kernel, final=final, tm_sub=TM_SUB),
        grid=(b, s // tm),
        in_specs=[
            tok(x), tok(yg), tok(ya),
            pl.BlockSpec((1, 1, N_MOD, d), lambda i, j: (0, i, 0, 0)),
            full(g_oa), full(w_out), full(g_ffn), full(w1), full(w2), full(g_fin),
        ],
        out_specs=tok(x),
        out_shape=jax.ShapeDtypeStruct(x.shape, F32),
        compiler_params=pltpu.CompilerParams(
            dimension_semantics=("arbitrary", "arbitrary"),
            vmem_limit_bytes=VMEM_LIMIT_BYTES),
        name="mix_out_final" if final else "mix_out",
    )(x, yg, ya, mod, g_oa, w_out, g_ffn, w1, w2, g_fin)


def _rotate_half_cols(w):
    half = w.shape[-1] // 2
    return jnp.concatenate([-w[..., half:], w[..., :half]], axis=-1)


def _prep_weights(w_in, mla_w_uq, mla_w_ukv, d_gmlp):
    depth, d, _ = w_in.shape
    q_rank = mla_w_uq.shape[1]
    kv_rank = mla_w_ukv.shape[1]
    o_kr = w_in.shape[-1] - QK_ROPE_DIM
    kr = w_in[..., o_kr:]
    w_uvt_in = jnp.swapaxes(w_in[..., :2 * d_gmlp], 1, 2).astype(BF16)
    w_in_ext = jnp.concatenate(
        [w_in[..., 2 * d_gmlp:o_kr], jnp.zeros((depth, d, QK_NOPE_DIM), F32), kr,
         _rotate_half_cols(kr)], axis=-1).astype(BF16)

    wq = mla_w_uq.reshape(depth, q_rank, MLA_HEADS, QK_NOPE_DIM + QK_ROPE_DIM)
    rope = wq[..., QK_NOPE_DIM:]
    w_uq_ext = jnp.concatenate([wq, _rotate_half_cols(rope)], axis=-1)
    w_uq_ext = w_uq_ext.reshape(depth, q_rank, MLA_HEADS * HEAD_SLAB).astype(BF16)

    wkv = mla_w_ukv.reshape(depth, kv_rank, MLA_HEADS, QK_NOPE_DIM + V_HEAD_DIM)
    w_k = jnp.concatenate(
        [wkv[..., :QK_NOPE_DIM],
         jnp.zeros((depth, kv_rank, MLA_HEADS, HEAD_SLAB - QK_NOPE_DIM), F32)], axis=-1)
    w_uk_ext = w_k.reshape(depth, kv_rank, MLA_HEADS * HEAD_SLAB).astype(BF16)
    w_v = jnp.concatenate(
        [wkv[..., QK_NOPE_DIM:], jnp.zeros((depth, kv_rank, MLA_HEADS, BF16_ROWS), F32)], axis=-1)
    w_uvt = jnp.swapaxes(w_v.reshape(depth, kv_rank, MLA_HEADS * V_SLAB), 1, 2).astype(BF16)
    v_one = (jnp.arange(MLA_HEADS * V_SLAB) % V_SLAB == V_HEAD_DIM).astype(F32).reshape(-1, 1)
    return w_uvt_in, w_in_ext, w_uq_ext, w_uk_ext, w_uvt, v_one


def kernel(x, c, positions, w_ada, b_ada, norm_mix_g, w_in, gmlp_ws, gmlp_bs, mla_q_norm_g,
           mla_kv_norm_g, mla_w_uq, mla_w_ukv, out_norm_gmlp_g, out_norm_mla_g, w_out,
           norm_ffn_g, w_ff1, w_ff2, final_norm_g):
    b, s, d = x.shape
    depth = w_ada.shape[0]
    d_gmlp = out_norm_gmlp_g.shape[-1]

    mod = _adaln(c, w_ada, b_ada).reshape(depth, b, N_MOD, d)
    cos_t, sin_t = _rope_tables(positions)
    w_uvt_in, w_in_ext, w_uq_ext, w_uk_ext, w_uvt, v_one = _prep_weights(
        w_in, mla_w_uq, mla_w_ukv, d_gmlp)
    wst = jnp.swapaxes(gmlp_ws, 2, 3)
    bias = gmlp_bs[:, :, None, :]
    g_og = out_norm_gmlp_g[:, :, None]
    w_out_b = w_out.astype(BF16)
    w1_b = w_ff1.astype(BF16)
    w2_b = w_ff2.astype(BF16)
    g_fin = final_norm_g.reshape(1, d)

    for l in range(depth):
        yg, q, k, vt = _mix_in(
            x, mod[l:l + 1], cos_t, sin_t, norm_mix_g[l:l + 1], w_uvt_in[l], w_in_ext[l], wst[l],
            bias[l], mla_q_norm_g[l:l + 1], mla_kv_norm_g[l:l + 1], w_uq_ext[l],
            w_uk_ext[l], w_uvt[l], v_one, g_og[l])
        ya = _attention(q, k, vt)
        x = _mix_out(x, yg, ya, mod[l:l + 1], out_norm_mla_g[l:l + 1], w_out_b[l],
                     norm_ffn_g[l:l + 1], w1_b[l], w2_b[l], g_fin, final=(l == depth - 1))
    return x
```

```python
import functools
import math

import jax
import jax.numpy as jnp
from jax import lax
from jax.experimental import pallas as pl
from jax.experimental.pallas import tpu as pltpu

F32 = jnp.float32
BF16 = jnp.bfloat16

EPS = 1e-6
ROPE_THETA = 10000.0
N_MOD = 6

GMLP_GROUPS = 8
CHUNK = 128
MLA_HEADS = 8
QK_NOPE_DIM = 64
QK_ROPE_DIM = 32
V_HEAD_DIM = 64
HEAD_SLAB = 128
LANES = 128
BF16_ROWS = 16
V_SLAB = V_HEAD_DIM + BF16_ROWS

VMEM_LIMIT_BYTES = 56 * 1024 * 1024

TM_IN = 1024
TM_SUB = 512
TM_OUT = 1024
TQ = 256
FF_CHUNK = 1024
ADA_TN = 1536


def _dot(a, b):
    return jnp.dot(a, b, preferred_element_type=F32)


def _rms(x):
    return x * lax.rsqrt(jnp.mean(x * x, axis=-1, keepdims=True) + EPS)


def _gelu_tanh(x):
    c = math.sqrt(2.0 / math.pi)
    half = 0.5 * x
    return half + half * jnp.tanh(x * (c + (c * 0.044715) * (x * x)))


def _adaln_kernel(c_ref, w_ref, b_ref, o_ref):
    c = c_ref[...]
    c_act = (c / (1.0 + jnp.exp(-c))).astype(BF16)
    o_ref[0] = _dot(c_act, w_ref[0].astype(BF16)) + b_ref[0]


def _adaln(c, w_ada, b_ada):
    depth, d, n = w_ada.shape
    b = c.shape[0]
    return pl.pallas_call(
        _adaln_kernel,
        grid=(depth, n // ADA_TN),
        in_specs=[
            pl.BlockSpec((b, d), lambda l, j: (0, 0)),
            pl.BlockSpec((1, d, ADA_TN), lambda l, j: (l, 0, j)),
            pl.BlockSpec((1, 1, ADA_TN), lambda l, j: (l, 0, j)),
        ],
        out_specs=pl.BlockSpec((1, b, ADA_TN), lambda l, j: (l, 0, j)),
        out_shape=jax.ShapeDtypeStruct((depth, b, n), F32),
        compiler_params=pltpu.CompilerParams(
            dimension_semantics=("arbitrary", "arbitrary"),
            vmem_limit_bytes=VMEM_LIMIT_BYTES),
        name="adaln_mod",
    )(c, w_ada, b_ada.reshape(depth, 1, n))


def _rope_kernel(pos_ref, freq_ref, cos_ref, sin_ref):
    pos = pos_ref[0].astype(F32)
    ang = freq_ref[...] * pos
    c, s = jnp.cos(ang), jnp.sin(ang)
    ts = pos.shape[1]
    pad = HEAD_SLAB - QK_NOPE_DIM - QK_ROPE_DIM
    cos_ref[0] = jnp.concatenate(
        [jnp.ones((QK_NOPE_DIM, ts), F32), c, c, jnp.zeros((pad, ts), F32)], axis=0).T
    sin_ref[0] = jnp.concatenate(
        [jnp.zeros((QK_NOPE_DIM, ts), F32), s, s, jnp.zeros((pad, ts), F32)], axis=0).T


def _rope_tables(positions):
    b, s = positions.shape
    freqs = ROPE_THETA ** (-jnp.arange(0, QK_ROPE_DIM, 2, dtype=F32) / QK_ROPE_DIM)
    ts = s
    out = jax.ShapeDtypeStruct((b, s, HEAD_SLAB), F32)
    return pl.pallas_call(
        _rope_kernel,
        grid=(b, s // ts),
        in_specs=[
            pl.BlockSpec((1, 1, ts), lambda i, j: (i, 0, j)),
            pl.BlockSpec((QK_ROPE_DIM // 2, 1), lambda i, j: (0, 0)),
        ],
        out_specs=[pl.BlockSpec((1, ts, HEAD_SLAB), lambda i, j: (i, j, 0))] * 2,
        out_shape=[out, out],
        compiler_params=pltpu.CompilerParams(
            dimension_semantics=("arbitrary", "arbitrary")),
        name="rope_tables",
    )(positions.reshape(b, 1, s), freqs.reshape(-1, 1))


def _mix_in_kernel(x_ref, mod_ref, cos_ref, sin_ref, g_mix_ref, w_uvt_in_ref, w_in_ref, wst_ref,
                   bias_ref, g_q_ref, g_kv_ref, w_uq_ref, w_uk_ref, w_uvt_ref, v_one_ref,
                   g_og_ref, yg_ref, q_ref, k_ref, vt_ref, *, d_gmlp, q_rank, kv_rank, q_scale, tm_sub):
    nt = (((1,), (1,)), ((), ()))
    mod = mod_ref[...]
    shift1, scale1 = mod[0:1], mod[1:2]
    gain1 = g_mix_ref[...] * (1.0 + scale1)
    n_chunk = tm_sub // CHUNK
    dg = d_gmlp // GMLP_GROUPS
    o_kv = q_rank
    o_kr = o_kv + kv_rank
    shift = HEAD_SLAB - QK_ROPE_DIM

    row = lax.broadcasted_iota(jnp.int32, (CHUNK, CHUNK), 0)
    col = lax.broadcasted_iota(jnp.int32, (CHUNK, CHUNK), 1)
    w_mix = [jnp.where(row <= col, wst_ref[g], 0.0).astype(BF16) for g in range(GMLP_GROUPS)]
    bias = [jnp.concatenate([bias_ref[g]] * n_chunk, axis=1) for g in range(GMLP_GROUPS)]

    n_sub = x_ref.shape[1] // tm_sub
    proj = []
    for sb in range(n_sub):
        rows = slice(sb * tm_sub, (sb + 1) * tm_sub)
        h = (_rms(x_ref[0, rows, :]) * gain1 + shift1).astype(BF16)
        proj.append((_dot(h, w_in_ref[...]),
                     lax.dot_general(w_uvt_in_ref[...], h, nt, preferred_element_type=F32)))

    for sb in range(n_sub):
        rows = slice(sb * tm_sub, (sb + 1) * tm_sub)
        z, uv_t = proj[sb]

        gu = _gelu_tanh(uv_t[:d_gmlp]).reshape(GMLP_GROUPS, dg, tm_sub)
        gv = _gelu_tanh(uv_t[d_gmlp:]).reshape(GMLP_GROUPS, dg, tm_sub)
        dv = gv - jnp.mean(gv, axis=1, keepdims=True)
        vn = (dv * lax.rsqrt(jnp.mean(dv * dv, axis=1, keepdims=True) + EPS)).astype(BF16)
        mixed = []
        for g in range(GMLP_GROUPS):
            lhs = jnp.concatenate([vn[g, :, c * CHUNK:(c + 1) * CHUNK] for c in range(n_chunk)], axis=0)
            out = _dot(lhs, w_mix[g])
            out = jnp.concatenate([out[c * dg:(c + 1) * dg] for c in range(n_chunk)], axis=1)
            mixed.append(out + bias[g])
        yg_t = (gu * jnp.stack(mixed)).reshape(d_gmlp, tm_sub)
        yg_t = yg_t * lax.rsqrt(jnp.mean(yg_t * yg_t, axis=0, keepdims=True) + EPS) * g_og_ref[...]
        yg_ref[0, rows, :] = yg_t.T.astype(BF16)

        cos_t = cos_ref[0, rows, :]
        sin_t = sin_ref[0, rows, :]
        c_q = (_rms(z[:, :o_kv]) * g_q_ref[...]).astype(BF16)
        qf = _dot(c_q, w_uq_ref[...])
        cos_q = cos_t * q_scale
        sin_q = sin_t * q_scale
        for hd in range(MLA_HEADS):
            qs = qf[:, hd * HEAD_SLAB:(hd + 1) * HEAD_SLAB]
            q_ref[0, hd, rows, :] = (qs * cos_q + pltpu.roll(qs, shift, 1) * sin_q).astype(BF16)

        c_kv = (_rms(z[:, o_kv:o_kr]) * g_kv_ref[...]).astype(BF16)
        kf = _dot(c_kv, w_uk_ref[...])
        kr = z[:, o_kr:]
        kr = kr * cos_t + pltpu.roll(kr, shift, 1) * sin_t
        for hd in range(MLA_HEADS):
            k_ref[0, hd, rows, :] = (kf[:, hd * HEAD_SLAB:(hd + 1) * HEAD_SLAB] + kr).astype(BF16)
        vt = lax.dot_general(w_uvt_ref[...], c_kv, nt,
                             preferred_element_type=F32)
        vt_ref[0, :, rows] = (vt + v_one_ref[...]).astype(BF16)


def _layer_spec(a, layer, **kw):
    return pl.BlockSpec((None,) + a.shape[1:], lambda i, j: (layer,) + (0,) * (a.ndim - 1), **kw)


def _mod_spec(mod, layer):
    return pl.BlockSpec((None, None) + mod.shape[2:], lambda i, j: (layer, i, 0, 0))


def _mix_in(layer, x, mod, cos_t, sin_t, g_mix, w_uvt_in, w_in, wst, bias, g_q, g_kv, w_uq, w_uk,
            w_uvt, v_one, g_og):
    b, s, d = x.shape
    d_gmlp = g_og.shape[1]
    q_rank = g_q.shape[-1]
    kv_rank = g_kv.shape[-1]
    tm = TM_IN
    q_scale = (QK_NOPE_DIM + QK_ROPE_DIM) ** -0.5 * math.log2(math.e)

    def par(a):
        return _layer_spec(a, layer)

    kern = functools.partial(_mix_in_kernel, d_gmlp=d_gmlp, q_rank=q_rank,
                             kv_rank=kv_rank, q_scale=q_scale, tm_sub=TM_SUB)
    slab = jax.ShapeDtypeStruct((b, MLA_HEADS, s, HEAD_SLAB), BF16)
    return pl.pallas_call(
        kern,
        grid=(b, s // tm),
        in_specs=[
            pl.BlockSpec((1, tm, d), lambda i, j: (i, j, 0)),
            _mod_spec(mod, layer),
            pl.BlockSpec((1, tm, HEAD_SLAB), lambda i, j: (i, j, 0)),
            pl.BlockSpec((1, tm, HEAD_SLAB), lambda i, j: (i, j, 0)),
            par(g_mix), par(w_uvt_in), par(w_in), par(wst), par(bias),
            par(g_q), par(g_kv), par(w_uq), par(w_uk), par(w_uvt),
            pl.BlockSpec(v_one.shape, lambda i, j: (0, 0)), par(g_og),
        ],
        out_specs=[
            pl.BlockSpec((1, tm, d_gmlp), lambda i, j: (i, j, 0)),
            pl.BlockSpec((1, MLA_HEADS, tm, HEAD_SLAB), lambda i, j: (i, 0, j, 0)),
            pl.BlockSpec((1, MLA_HEADS, tm, HEAD_SLAB), lambda i, j: (i, 0, j, 0)),
            pl.BlockSpec((1, MLA_HEADS * V_SLAB, tm), lambda i, j: (i, 0, j)),
        ],
        out_shape=[
            jax.ShapeDtypeStruct((b, s, d_gmlp), BF16),
            slab, slab,
            jax.ShapeDtypeStruct((b, MLA_HEADS * V_SLAB, s), BF16),
        ],
        compiler_params=pltpu.CompilerParams(
            dimension_semantics=("arbitrary", "arbitrary"),
            vmem_limit_bytes=VMEM_LIMIT_BYTES),
        name="mix_in",
    )(x, mod, cos_t, sin_t, g_mix, w_uvt_in, w_in, wst, bias, g_q, g_kv, w_uq, w_uk, w_uvt, v_one,
      g_og)


def _attn_kernel(q_ref, k_ref, vt_ref, o_ref, *, tq):
    s_len = q_ref.shape[2]
    nt = (((1,), (1,)), ((), ()))
    key = lax.broadcasted_iota(jnp.int32, (tq, tq), 0)
    qry = lax.broadcasted_iota(jnp.int32, (tq, tq), 1)
    causal = key <= qry
    n_blk = s_len // tq

    def qk(c):
        q0, kend = c * tq, (c + 1) * tq
        return [lax.dot_general(k_ref[0, e, :kend, :], q_ref[0, e, q0:kend, :], nt,
                                preferred_element_type=F32) for e in range(2)]

    nxt = qk(0)
    for c in range(n_blk):
        q0, kend = c * tq, (c + 1) * tq
        scores = nxt
        if c + 1 < n_blk:
            nxt = qk(c + 1)
        halves = []
        for e in range(2):
            s = scores[e]
            diag = jnp.where(causal, s[q0:], -1e30)
            s = diag if c == 0 else jnp.concatenate([s[:q0], diag], axis=0)
            p = jnp.exp2(s - jnp.max(s, axis=0, keepdims=True)).astype(BF16)
            o = _dot(vt_ref[0, e * V_SLAB:(e + 1) * V_SLAB, :kend], p)
            halves.append(o[:V_HEAD_DIM] * (1.0 / o[V_HEAD_DIM:V_HEAD_DIM + 1]))
        o_ref[0, q0:kend, :] = jnp.concatenate(halves, axis=0).T.astype(o_ref.dtype)


def _attention(q, k, vt):
    b, hds, s, _ = q.shape
    return pl.pallas_call(
        functools.partial(_attn_kernel, tq=TQ),
        grid=(b, hds // 2),
        in_specs=[
            pl.BlockSpec((1, 2, s, HEAD_SLAB), lambda bi, j: (bi, j, 0, 0)),
            pl.BlockSpec((1, 2, s, HEAD_SLAB), lambda bi, j: (bi, j, 0, 0)),
            pl.BlockSpec((1, 2 * V_SLAB, s), lambda bi, j: (bi, j, 0)),
        ],
        out_specs=pl.BlockSpec((1, s, LANES), lambda bi, j: (bi, 0, j)),
        out_shape=jax.ShapeDtypeStruct((b, s, hds * V_HEAD_DIM), BF16),
        compiler_params=pltpu.CompilerParams(
            dimension_semantics=("arbitrary", "arbitrary"),
            vmem_limit_bytes=VMEM_LIMIT_BYTES),
        name="attn",
    )(q, k, vt)


def _mix_out_kernel(x_ref, yg_ref, ya_ref, mod_ref, g_oa_ref, w_out_ref, g_ffn_ref,
                    w1_ref, w2_ref, g_fin_ref, o_ref, *, final, tm_sub):
    mod = mod_ref[...]
    gate1, shift2, scale2, gate2 = mod[2:3], mod[3:4], mod[4:5], mod[5:6]
    gain2 = g_ffn_ref[...] * (1.0 + scale2)
    d_g = yg_ref.shape[-1]
    n_sub = x_ref.shape[1] // tm_sub

    mids = []
    for sb in range(n_sub):
        rows = slice(sb * tm_sub, (sb + 1) * tm_sub)
        ya = (_rms(ya_ref[0, rows, :].astype(F32)) * g_oa_ref[...]).astype(BF16)
        mixed = _dot(yg_ref[0, rows, :], w_out_ref[:d_g, :]) + _dot(ya, w_out_ref[d_g:, :])
        x1 = x_ref[0, rows, :] + gate1 * mixed
        mids.append((x1, (_rms(x1) * gain2 + shift2).astype(BF16)))

    for sb in range(n_sub):
        rows = slice(sb * tm_sub, (sb + 1) * tm_sub)
        x1, h = mids[sb]
        f = jnp.zeros_like(x1)
        for c in range(w1_ref.shape[1] // FF_CHUNK):
            a = jnp.maximum(_dot(h, w1_ref[:, c * FF_CHUNK:(c + 1) * FF_CHUNK]), 0.0)
            f = f + _dot((a * a).astype(BF16), w2_ref[c * FF_CHUNK:(c + 1) * FF_CHUNK, :])
        x2 = x1 + gate2 * f
        if final:
            x2 = _rms(x2) * g_fin_ref[...]
        o_ref[0, rows, :] = x2


def _mix_out(layer, x, yg, ya, mod, g_oa, w_out, g_ffn, w1, w2, g_fin, final):
    b, s, d = x.shape
    tm = TM_OUT

    def par(a):
        return _layer_spec(a, layer, pipeline_mode=pl.Buffered(1))

    def tok(a):
        return pl.BlockSpec((1, tm, a.shape[-1]), lambda i, j: (i, j, 0))

    return pl.pallas_call(
        functools.partial(_mix_out_kernel, final=final, tm_sub=TM_SUB),
        grid=(b, s // tm),
        in_specs=[
            tok(x), tok(yg), tok(ya),
            _mod_spec(mod, layer),
            par(g_oa), par(w_out), par(g_ffn), par(w1), par(w2),
            pl.BlockSpec(g_fin.shape, lambda i, j: (0, 0), pipeline_mode=pl.Buffered(1)),
        ],
        out_specs=tok(x),
        out_shape=jax.ShapeDtypeStruct(x.shape, F32),
        compiler_params=pltpu.CompilerParams(
            dimension_semantics=("arbitrary", "arbitrary"),
            vmem_limit_bytes=VMEM_LIMIT_BYTES),
        name="mix_out_final" if final else "mix_out",
    )(x, yg, ya, mod, g_oa, w_out, g_ffn, w1, w2, g_fin)


def _rotate_half_cols(w):
    half = w.shape[-1] // 2
    return jnp.concatenate([-w[..., half:], w[..., :half]], axis=-1)


def _prep_weights(w_in, mla_w_uq, mla_w_ukv, d_gmlp):
    depth, d, _ = w_in.shape
    q_rank = mla_w_uq.shape[1]
    kv_rank = mla_w_ukv.shape[1]
    o_kr = w_in.shape[-1] - QK_ROPE_DIM
    kr = w_in[..., o_kr:]
    w_uvt_in = jnp.swapaxes(w_in[..., :2 * d_gmlp], 1, 2).astype(BF16)
    w_in_ext = jnp.concatenate(
        [w_in[..., 2 * d_gmlp:o_kr], jnp.zeros((depth, d, QK_NOPE_DIM), F32), kr,
         _rotate_half_cols(kr)], axis=-1).astype(BF16)

    wq = mla_w_uq.reshape(depth, q_rank, MLA_HEADS, QK_NOPE_DIM + QK_ROPE_DIM)
    rope = wq[..., QK_NOPE_DIM:]
    w_uq_ext = jnp.concatenate([wq, _rotate_half_cols(rope)], axis=-1)
    w_uq_ext = w_uq_ext.reshape(depth, q_rank, MLA_HEADS * HEAD_SLAB).astype(BF16)

    wkv = mla_w_ukv.reshape(depth, kv_rank, MLA_HEADS, QK_NOPE_DIM + V_HEAD_DIM)
    w_k = jnp.concatenate(
        [wkv[..., :QK_NOPE_DIM],
         jnp.zeros((depth, kv_rank, MLA_HEADS, HEAD_SLAB - QK_NOPE_DIM), F32)], axis=-1)
    w_uk_ext = w_k.reshape(depth, kv_rank, MLA_HEADS * HEAD_SLAB).astype(BF16)
    w_v = jnp.concatenate(
        [wkv[..., QK_NOPE_DIM:], jnp.zeros((depth, kv_rank, MLA_HEADS, BF16_ROWS), F32)], axis=-1)
    w_uvt = jnp.swapaxes(w_v.reshape(depth, kv_rank, MLA_HEADS * V_SLAB), 1, 2).astype(BF16)
    v_one = (jnp.arange(MLA_HEADS * V_SLAB) % V_SLAB == V_HEAD_DIM).astype(F32).reshape(-1, 1)
    return w_uvt_in, w_in_ext, w_uq_ext, w_uk_ext, w_uvt, v_one


def kernel(x, c, positions, w_ada, b_ada, norm_mix_g, w_in, gmlp_ws, gmlp_bs, mla_q_norm_g,
           mla_kv_norm_g, mla_w_uq, mla_w_ukv, out_norm_gmlp_g, out_norm_mla_g, w_out,
           norm_ffn_g, w_ff1, w_ff2, final_norm_g):
    b, s, d = x.shape
    depth = w_ada.shape[0]
    d_gmlp = out_norm_gmlp_g.shape[-1]

    mod = _adaln(c, w_ada, b_ada).reshape(depth, b, N_MOD, d)
    cos_t, sin_t = _rope_tables(positions)
    w_uvt_in, w_in_ext, w_uq_ext, w_uk_ext, w_uvt, v_one = _prep_weights(
        w_in, mla_w_uq, mla_w_ukv, d_gmlp)
    wst = jnp.swapaxes(gmlp_ws, 2, 3)
    bias = gmlp_bs[:, :, None, :]
    g_og = out_norm_gmlp_g[:, :, None]
    w_out_b = w_out.astype(BF16)
    w1_b = w_ff1.astype(BF16)
    w2_b = w_ff2.astype(BF16)
    g_fin = final_norm_g.reshape(1, d)

    def rows(g):
        return g[:, None, :]

    for l in range(depth):
        yg, q, k, vt = _mix_in(
            l, x, mod, cos_t, sin_t, rows(norm_mix_g), w_uvt_in, w_in_ext, wst, bias,
            rows(mla_q_norm_g), rows(mla_kv_norm_g), w_uq_ext, w_uk_ext, w_uvt, v_one, g_og)
        ya = _attention(q, k, vt)
        x = _mix_out(l, x, yg, ya, mod, rows(out_norm_mla_g), w_out_b, rows(norm_ffn_g),
                     w1_b, w2_b, g_fin, final=(l == depth - 1))
    return x
```

```python
import functools
import math

import jax
import jax.numpy as jnp
from jax import lax
from jax.experimental import pallas as pl
from jax.experimental.pallas import tpu as pltpu

F32 = jnp.float32
BF16 = jnp.bfloat16

EPS = 1e-6
ROPE_THETA = 10000.0
N_MOD = 6

GMLP_GROUPS = 8
CHUNK = 128
MLA_HEADS = 8
QK_NOPE_DIM = 64
QK_ROPE_DIM = 32
V_HEAD_DIM = 64
HEAD_SLAB = 128
LANES = 128
BF16_ROWS = 16
V_SLAB = V_HEAD_DIM + BF16_ROWS

VMEM_LIMIT_BYTES = 56 * 1024 * 1024

TM_IN = 1024
TM_SUB = 512
TM_OUT = 1024
TQ = 512
FF_CHUNK = 1024
ADA_TN = 1536


def _dot(a, b):
    return jnp.dot(a, b, preferred_element_type=F32)


def _rms(x):
    return x * lax.rsqrt(jnp.mean(x * x, axis=-1, keepdims=True) + EPS)


def _gelu_tanh(x):
    c = math.sqrt(2.0 / math.pi)
    half = 0.5 * x
    return half + half * jnp.tanh(x * (c + (c * 0.044715) * (x * x)))


def _adaln_kernel(c_ref, w_ref, b_ref, o_ref):
    c = c_ref[...]
    c_act = (c / (1.0 + jnp.exp(-c))).astype(BF16)
    o_ref[0] = _dot(c_act, w_ref[0].astype(BF16)) + b_ref[0]


def _adaln(c, w_ada, b_ada):
    depth, d, n = w_ada.shape
    b = c.shape[0]
    return pl.pallas_call(
        _adaln_kernel,
        grid=(depth, n // ADA_TN),
        in_specs=[
            pl.BlockSpec((b, d), lambda l, j: (0, 0)),
            pl.BlockSpec((1, d, ADA_TN), lambda l, j: (l, 0, j)),
            pl.BlockSpec((1, 1, ADA_TN), lambda l, j: (l, 0, j)),
        ],
        out_specs=pl.BlockSpec((1, b, ADA_TN), lambda l, j: (l, 0, j)),
        out_shape=jax.ShapeDtypeStruct((depth, b, n), F32),
        compiler_params=pltpu.CompilerParams(
            dimension_semantics=("arbitrary", "arbitrary"),
            vmem_limit_bytes=VMEM_LIMIT_BYTES),
        name="adaln_mod",
    )(c, w_ada, b_ada.reshape(depth, 1, n))


def _rope_kernel(pos_ref, freq_ref, cos_ref, sin_ref):
    pos = pos_ref[0].astype(F32)
    ang = freq_ref[...] * pos
    c, s = jnp.cos(ang), jnp.sin(ang)
    ts = pos.shape[1]
    pad = HEAD_SLAB - QK_NOPE_DIM - QK_ROPE_DIM
    cos_ref[0] = jnp.concatenate(
        [jnp.ones((QK_NOPE_DIM, ts), F32), c, c, jnp.zeros((pad, ts), F32)], axis=0).T
    sin_ref[0] = jnp.concatenate(
        [jnp.zeros((QK_NOPE_DIM, ts), F32), s, s, jnp.zeros((pad, ts), F32)], axis=0).T


def _rope_tables(positions):
    b, s = positions.shape
    freqs = ROPE_THETA ** (-jnp.arange(0, QK_ROPE_DIM, 2, dtype=F32) / QK_ROPE_DIM)
    ts = s
    out = jax.ShapeDtypeStruct((b, s, HEAD_SLAB), F32)
    return pl.pallas_call(
        _rope_kernel,
        grid=(b, s // ts),
        in_specs=[
            pl.BlockSpec((1, 1, ts), lambda i, j: (i, 0, j)),
            pl.BlockSpec((QK_ROPE_DIM // 2, 1), lambda i, j: (0, 0)),
        ],
        out_specs=[pl.BlockSpec((1, ts, HEAD_SLAB), lambda i, j: (i, j, 0))] * 2,
        out_shape=[out, out],
        compiler_params=pltpu.CompilerParams(
            dimension_semantics=("arbitrary", "arbitrary")),
        name="rope_tables",
    )(positions.reshape(b, 1, s), freqs.reshape(-1, 1))


def _mix_in_kernel(x_ref, mod_ref, cos_ref, sin_ref, g_mix_ref, w_uvt_in_ref, w_in_ref, wst_ref,
                   bias_ref, g_q_ref, g_kv_ref, w_uq_ref, w_uk_ref, w_uvt_ref, v_one_ref,
                   g_og_ref, yg_ref, q_ref, k_ref, vt_ref, *, d_gmlp, q_rank, kv_rank, q_scale, tm_sub):
    nt = (((1,), (1,)), ((), ()))
    mod = mod_ref[...]
    shift1, scale1 = mod[0:1], mod[1:2]
    gain1 = g_mix_ref[...] * (1.0 + scale1)
    n_chunk = tm_sub // CHUNK
    dg = d_gmlp // GMLP_GROUPS
    o_kv = q_rank
    o_kr = o_kv + kv_rank
    shift = HEAD_SLAB - QK_ROPE_DIM

    row = lax.broadcasted_iota(jnp.int32, (CHUNK, CHUNK), 0)
    col = lax.broadcasted_iota(jnp.int32, (CHUNK, CHUNK), 1)
    w_mix = [jnp.where(row <= col, wst_ref[g], 0.0).astype(BF16) for g in range(GMLP_GROUPS)]
    bias = [jnp.concatenate([bias_ref[g]] * n_chunk, axis=1) for g in range(GMLP_GROUPS)]

    n_sub = x_ref.shape[1] // tm_sub
    proj = []
    for sb in range(n_sub):
        rows = slice(sb * tm_sub, (sb + 1) * tm_sub)
        h = (_rms(x_ref[0, rows, :]) * gain1 + shift1).astype(BF16)
        proj.append((_dot(h, w_in_ref[...]),
                     lax.dot_general(w_uvt_in_ref[...], h, nt, preferred_element_type=F32)))

    def mla(sb):
        rows = slice(sb * tm_sub, (sb + 1) * tm_sub)
        z = proj[sb][0]
        cos_t = cos_ref[0, rows, :]
        sin_t = sin_ref[0, rows, :]
        c_q = (_rms(z[:, :o_kv]) * g_q_ref[...]).astype(BF16)
        qf = _dot(c_q, w_uq_ref[...])
        cos_q = cos_t * q_scale
        sin_q = sin_t * q_scale
        for hd in range(MLA_HEADS):
            qs = qf[:, hd * HEAD_SLAB:(hd + 1) * HEAD_SLAB]
            q_ref[0, hd, rows, :] = (qs * cos_q + pltpu.roll(qs, shift, 1) * sin_q).astype(BF16)

        c_kv = (_rms(z[:, o_kv:o_kr]) * g_kv_ref[...]).astype(BF16)
        kf = _dot(c_kv, w_uk_ref[...])
        kr = z[:, o_kr:]
        kr = kr * cos_t + pltpu.roll(kr, shift, 1) * sin_t
        for hd in range(MLA_HEADS):
            k_ref[0, hd, rows, :] = (kf[:, hd * HEAD_SLAB:(hd + 1) * HEAD_SLAB] + kr).astype(BF16)
        vt = lax.dot_general(w_uvt_ref[...], c_kv, nt,
                             preferred_element_type=F32)
        vt_ref[0, :, rows] = (vt + v_one_ref[...]).astype(BF16)

    def gmlp(sb):
        rows = slice(sb * tm_sub, (sb + 1) * tm_sub)
        uv_t = proj[sb][1]
        gu = _gelu_tanh(uv_t[:d_gmlp]).reshape(GMLP_GROUPS, dg, tm_sub)
        gv = _gelu_tanh(uv_t[d_gmlp:]).reshape(GMLP_GROUPS, dg, tm_sub)
        dv = gv - jnp.mean(gv, axis=1, keepdims=True)
        vn = (dv * lax.rsqrt(jnp.mean(dv * dv, axis=1, keepdims=True) + EPS)).astype(BF16)
        mixed = []
        for g in range(GMLP_GROUPS):
            lhs = jnp.concatenate([vn[g, :, c * CHUNK:(c + 1) * CHUNK] for c in range(n_chunk)], axis=0)
            out = _dot(lhs, w_mix[g])
            out = jnp.concatenate([out[c * dg:(c + 1) * dg] for c in range(n_chunk)], axis=1)
            mixed.append(out + bias[g])
        yg_t = (gu * jnp.stack(mixed)).reshape(d_gmlp, tm_sub)
        yg_t = yg_t * lax.rsqrt(jnp.mean(yg_t * yg_t, axis=0, keepdims=True) + EPS) * g_og_ref[...]
        yg_ref[0, rows, :] = yg_t.T.astype(BF16)

    for sb in range(n_sub):
        gmlp(sb)
    for sb in range(n_sub):
        mla(sb)


def _layer_spec(a, layer, **kw):
    return pl.BlockSpec((None,) + a.shape[1:], lambda i, j: (layer,) + (0,) * (a.ndim - 1), **kw)


def _mod_spec(mod, layer):
    return pl.BlockSpec((None, None) + mod.shape[2:], lambda i, j: (layer, i, 0, 0))


def _mix_in(layer, x, mod, cos_t, sin_t, g_mix, w_uvt_in, w_in, wst, bias, g_q, g_kv, w_uq, w_uk,
            w_uvt, v_one, g_og):
    b, s, d = x.shape
    d_gmlp = g_og.shape[1]
    q_rank = g_q.shape[-1]
    kv_rank = g_kv.shape[-1]
    tm = TM_IN
    q_scale = (QK_NOPE_DIM + QK_ROPE_DIM) ** -0.5 * math.log2(math.e)

    def par(a):
        return _layer_spec(a, layer)

    kern = functools.partial(_mix_in_kernel, d_gmlp=d_gmlp, q_rank=q_rank,
                             kv_rank=kv_rank, q_scale=q_scale, tm_sub=TM_SUB)
    slab = jax.ShapeDtypeStruct((b, MLA_HEADS, s, HEAD_SLAB), BF16)
    return pl.pallas_call(
        kern,
        grid=(b, s // tm),
        in_specs=[
            pl.BlockSpec((1, tm, d), lambda i, j: (i, j, 0)),
            _mod_spec(mod, layer),
            pl.BlockSpec((1, tm, HEAD_SLAB), lambda i, j: (i, j, 0)),
            pl.BlockSpec((1, tm, HEAD_SLAB), lambda i, j: (i, j, 0)),
            par(g_mix), par(w_uvt_in), par(w_in), par(wst), par(bias),
            par(g_q), par(g_kv), par(w_uq), par(w_uk), par(w_uvt),
            pl.BlockSpec(v_one.shape, lambda i, j: (0, 0)), par(g_og),
        ],
        out_specs=[
            pl.BlockSpec((1, tm, d_gmlp), lambda i, j: (i, j, 0)),
            pl.BlockSpec((1, MLA_HEADS, tm, HEAD_SLAB), lambda i, j: (i, 0, j, 0)),
            pl.BlockSpec((1, MLA_HEADS, tm, HEAD_SLAB), lambda i, j: (i, 0, j, 0)),
            pl.BlockSpec((1, MLA_HEADS * V_SLAB, tm), lambda i, j: (i, 0, j)),
        ],
        out_shape=[
            jax.ShapeDtypeStruct((b, s, d_gmlp), BF16),
            slab, slab,
            jax.ShapeDtypeStruct((b, MLA_HEADS * V_SLAB, s), BF16),
        ],
        compiler_params=pltpu.CompilerParams(
            dimension_semantics=("arbitrary", "arbitrary"),
            vmem_limit_bytes=VMEM_LIMIT_BYTES),
        name="mix_in",
    )(x, mod, cos_t, sin_t, g_mix, w_uvt_in, w_in, wst, bias, g_q, g_kv, w_uq, w_uk, w_uvt, v_one,
      g_og)


def _attn_kernel(q_ref, k_ref, vt_ref, o_ref, *, tq):
    s_len = q_ref.shape[2]
    nt = (((1,), (1,)), ((), ()))
    half = tq // 2
    key = lax.broadcasted_iota(jnp.int32, (half, half), 0)
    qry = lax.broadcasted_iota(jnp.int32, (half, half), 1)
    causal = key <= qry
    n_blk = s_len // tq

    def qk(c):
        q0, kmid, kend = c * tq, c * tq + half, (c + 1) * tq
        return [(lax.dot_general(k_ref[0, e, :kmid, :], q_ref[0, e, q0:kend, :], nt,
                                 preferred_element_type=F32),
                 lax.dot_general(k_ref[0, e, kmid:kend, :], q_ref[0, e, kmid:kend, :], nt,
                                 preferred_element_type=F32))
                for e in range(2)]

    nxt = qk(0)
    for c in range(n_blk):
        q0, kmid, kend = c * tq, c * tq + half, (c + 1) * tq
        scores = nxt
        if c + 1 < n_blk:
            nxt = qk(c + 1)
        halves = []
        for e in range(2):
            a, b = scores[e]
            top = jnp.concatenate([jnp.where(causal, a[q0:, :half], -1e30), a[q0:, half:]], axis=1)
            a = top if c == 0 else jnp.concatenate([a[:q0], top], axis=0)
            b = jnp.where(causal, b, -1e30)
            m = jnp.max(a, axis=0, keepdims=True)
            m = jnp.concatenate(
                [m[:, :half], jnp.maximum(m[:, half:], jnp.max(b, axis=0, keepdims=True))], axis=1)
            p_a = jnp.exp2(a - m).astype(BF16)
            p_b = jnp.exp2(b - m[:, half:]).astype(BF16)
            vt = vt_ref.at[0, e * V_SLAB:(e + 1) * V_SLAB, :]
            o = _dot(vt[:, :kmid], p_a)
            o = jnp.concatenate([o[:, :half], o[:, half:] + _dot(vt[:, kmid:kend], p_b)], axis=1)
            halves.append(o[:V_HEAD_DIM] * (1.0 / o[V_HEAD_DIM:V_HEAD_DIM + 1]))
        o_ref[0, q0:kend, :] = jnp.concatenate(halves, axis=0).T.astype(o_ref.dtype)


def _attention(q, k, vt):
    b, hds, s, _ = q.shape
    return pl.pallas_call(
        functools.partial(_attn_kernel, tq=TQ),
        grid=(b, hds // 2),
        in_specs=[
            pl.BlockSpec((1, 2, s, HEAD_SLAB), lambda bi, j: (bi, j, 0, 0)),
            pl.BlockSpec((1, 2, s, HEAD_SLAB), lambda bi, j: (bi, j, 0, 0)),
            pl.BlockSpec((1, 2 * V_SLAB, s), lambda bi, j: (bi, j, 0)),
        ],
        out_specs=pl.BlockSpec((1, s, LANES), lambda bi, j: (bi, 0, j)),
        out_shape=jax.ShapeDtypeStruct((b, s, hds * V_HEAD_DIM), BF16),
        compiler_params=pltpu.CompilerParams(
            dimension_semantics=("arbitrary", "arbitrary"),
            vmem_limit_bytes=VMEM_LIMIT_BYTES),
        name="attn",
    )(q, k, vt)


def _mix_out_kernel(x_ref, yg_ref, ya_ref, mod_ref, g_oa_ref, w_out_ref, g_ffn_ref,
                    w1_ref, w2_ref, g_fin_ref, o_ref, *, final, tm_sub):
    mod = mod_ref[...]
    gate1, shift2, scale2, gate2 = mod[2:3], mod[3:4], mod[4:5], mod[5:6]
    gain2 = g_ffn_ref[...] * (1.0 + scale2)
    d_g = yg_ref.shape[-1]
    n_sub = x_ref.shape[1] // tm_sub

    mids = []
    for sb in range(n_sub):
        rows = slice(sb * tm_sub, (sb + 1) * tm_sub)
        ya = (_rms(ya_ref[0, rows, :].astype(F32)) * g_oa_ref[...]).astype(BF16)
        mixed = _dot(yg_ref[0, rows, :], w_out_ref[:d_g, :]) + _dot(ya, w_out_ref[d_g:, :])
        x1 = x_ref[0, rows, :] + gate1 * mixed
        mids.append((x1, (_rms(x1) * gain2 + shift2).astype(BF16)))

    for sb in range(n_sub):
        rows = slice(sb * tm_sub, (sb + 1) * tm_sub)
        x1, h = mids[sb]
        f = jnp.zeros_like(x1)
        for c in range(w1_ref.shape[1] // FF_CHUNK):
            a = jnp.maximum(_dot(h, w1_ref[:, c * FF_CHUNK:(c + 1) * FF_CHUNK]), 0.0)
            f = f + _dot((a * a).astype(BF16), w2_ref[c * FF_CHUNK:(c + 1) * FF_CHUNK, :])
        x2 = x1 + gate2 * f
        if final:
            x2 = _rms(x2) * g_fin_ref[...]
        o_ref[0, rows, :] = x2


def _mix_out(layer, x, yg, ya, mod, g_oa, w_out, g_ffn, w1, w2, g_fin, final):
    b, s, d = x.shape
    tm = TM_OUT

    def par(a):
        return _layer_spec(a, layer, pipeline_mode=pl.Buffered(1))

    def tok(a):
        return pl.BlockSpec((1, tm, a.shape[-1]), lambda i, j: (i, j, 0))

    return pl.pallas_call(
        functools.partial(_mix_out_kernel, final=final, tm_sub=TM_SUB),
        grid=(b, s // tm),
        in_specs=[
            tok(x), tok(yg), tok(ya),
            _mod_spec(mod, layer),
            par(g_oa), par(w_out), par(g_ffn), par(w1), par(w2),
            pl.BlockSpec(g_fin.shape, lambda i, j: (0, 0), pipeline_mode=pl.Buffered(1)),
        ],
        out_specs=tok(x),
        out_shape=jax.ShapeDtypeStruct(x.shape, F32),
        compiler_params=pltpu.CompilerParams(
            dimension_semantics=("arbitrary", "arbitrary"),
            vmem_limit_bytes=VMEM_LIMIT_BYTES),
        name="mix_out_final" if final else "mix_out",
    )(x, yg, ya, mod, g_oa, w_out, g_ffn, w1, w2, g_fin)


def _rotate_half_cols(w):
    half = w.shape[-1] // 2
    return jnp.concatenate([-w[..., half:], w[..., :half]], axis=-1)


def _prep_weights(w_in, mla_w_uq, mla_w_ukv, d_gmlp):
    depth, d, _ = w_in.shape
    q_rank = mla_w_uq.shape[1]
    kv_rank = mla_w_ukv.shape[1]
    o_kr = w_in.shape[-1] - QK_ROPE_DIM
    kr = w_in[..., o_kr:]
    w_uvt_in = jnp.swapaxes(w_in[..., :2 * d_gmlp], 1, 2).astype(BF16)
    w_in_ext = jnp.concatenate(
        [w_in[..., 2 * d_gmlp:o_kr], jnp.zeros((depth, d, QK_NOPE_DIM), F32), kr,
         _rotate_half_cols(kr)], axis=-1).astype(BF16)

    wq = mla_w_uq.reshape(depth, q_rank, MLA_HEADS, QK_NOPE_DIM + QK_ROPE_DIM)
    rope = wq[..., QK_NOPE_DIM:]
    w_uq_ext = jnp.concatenate([wq, _rotate_half_cols(rope)], axis=-1)
    w_uq_ext = w_uq_ext.reshape(depth, q_rank, MLA_HEADS * HEAD_SLAB).astype(BF16)

    wkv = mla_w_ukv.reshape(depth, kv_rank, MLA_HEADS, QK_NOPE_DIM + V_HEAD_DIM)
    w_k = jnp.concatenate(
        [wkv[..., :QK_NOPE_DIM],
         jnp.zeros((depth, kv_rank, MLA_HEADS, HEAD_SLAB - QK_NOPE_DIM), F32)], axis=-1)
    w_uk_ext = w_k.reshape(depth, kv_rank, MLA_HEADS * HEAD_SLAB).astype(BF16)
    w_v = jnp.concatenate(
        [wkv[..., QK_NOPE_DIM:], jnp.zeros((depth, kv_rank, MLA_HEADS, BF16_ROWS), F32)], axis=-1)
    w_uvt = jnp.swapaxes(w_v.reshape(depth, kv_rank, MLA_HEADS * V_SLAB), 1, 2).astype(BF16)
    v_one = (jnp.arange(MLA_HEADS * V_SLAB) % V_SLAB == V_HEAD_DIM).astype(F32).reshape(-1, 1)
    return w_uvt_in, w_in_ext, w_uq_ext, w_uk_ext, w_uvt, v_one


def kernel(x, c, positions, w_ada, b_ada, norm_mix_g, w_in, gmlp_ws, gmlp_bs, mla_q_norm_g,
           mla_kv_norm_g, mla_w_uq, mla_w_ukv, out_norm_gmlp_g, out_norm_mla_g, w_out,
           norm_ffn_g, w_ff1, w_ff2, final_norm_g):
    b, s, d = x.shape
    depth = w_ada.shape[0]
    d_gmlp = out_norm_gmlp_g.shape[-1]

    mod = _adaln(c, w_ada, b_ada).reshape(depth, b, N_MOD, d)
    cos_t, sin_t = _rope_tables(positions)
    w_uvt_in, w_in_ext, w_uq_ext, w_uk_ext, w_uvt, v_one = _prep_weights(
        w_in, mla_w_uq, mla_w_ukv, d_gmlp)
    wst = jnp.swapaxes(gmlp_ws, 2, 3)
    bias = gmlp_bs[:, :, None, :]
    g_og = out_norm_gmlp_g[:, :, None]
    w_out_b = w_out.astype(BF16)
    w1_b = w_ff1.astype(BF16)
    w2_b = w_ff2.astype(BF16)
    g_fin = final_norm_g.reshape(1, d)

    def rows(g):
        return g[:, None, :]

    for l in range(depth):
        yg, q, k, vt = _mix_in(
            l, x, mod, cos_t, sin_t, rows(norm_mix_g), w_uvt_in, w_in_ext, wst, bias,
            rows(mla_q_norm_g), rows(mla_kv_norm_g), w_uq_ext, w_uk_ext, w_uvt, v_one, g_og)
        ya = _attention(q, k, vt)
        x = _mix_out(l, x, yg, ya, mod, rows(out_norm_mla_g), w_out_b, rows(norm_ffn_g),
                     w1_b, w2_b, g_fin, final=(l == depth - 1))
    return x
```

```python
import functools
import math

import jax
import jax.numpy as jnp
from jax import lax
from jax.experimental import pallas as pl
from jax.experimental.pallas import tpu as pltpu

F32 = jnp.float32
BF16 = jnp.bfloat16

EPS = 1e-6
ROPE_THETA = 10000.0
N_MOD = 6

GMLP_GROUPS = 8
CHUNK = 128
MLA_HEADS = 8
QK_NOPE_DIM = 64
QK_ROPE_DIM = 32
V_HEAD_DIM = 64
HEAD_SLAB = 128
LANES = 128
BF16_ROWS = 16
V_SLAB = V_HEAD_DIM + BF16_ROWS

VMEM_LIMIT_BYTES = 56 * 1024 * 1024

TM_IN = 1024
TM_SUB = 512
TM_OUT = 1024
TQ = 512
ATTN_HEADS_PER_STEP = 4
FF_CHUNK = 1024
ADA_TN = 1536


def _dot(a, b):
    return jnp.dot(a, b, preferred_element_type=F32)


def _rms(x):
    return x * lax.rsqrt(jnp.mean(x * x, axis=-1, keepdims=True) + EPS)


def _gelu_tanh(x):
    c = math.sqrt(2.0 / math.pi)
    half = 0.5 * x
    return half + half * jnp.tanh(x * (c + (c * 0.044715) * (x * x)))


def _adaln_kernel(c_ref, w_ref, b_ref, o_ref):
    c = c_ref[...]
    c_act = (c / (1.0 + jnp.exp(-c))).astype(BF16)
    o_ref[0] = _dot(c_act, w_ref[0].astype(BF16)) + b_ref[0]


def _adaln(c, w_ada, b_ada):
    depth, d, n = w_ada.shape
    b = c.shape[0]
    return pl.pallas_call(
        _adaln_kernel,
        grid=(depth, n // ADA_TN),
        in_specs=[
            pl.BlockSpec((b, d), lambda l, j: (0, 0)),
            pl.BlockSpec((1, d, ADA_TN), lambda l, j: (l, 0, j)),
            pl.BlockSpec((1, 1, ADA_TN), lambda l, j: (l, 0, j)),
        ],
        out_specs=pl.BlockSpec((1, b, ADA_TN), lambda l, j: (l, 0, j)),
        out_shape=jax.ShapeDtypeStruct((depth, b, n), F32),
        compiler_params=pltpu.CompilerParams(
            dimension_semantics=("arbitrary", "arbitrary"),
            vmem_limit_bytes=VMEM_LIMIT_BYTES),
        name="adaln_mod",
    )(c, w_ada, b_ada.reshape(depth, 1, n))


def _rope_kernel(pos_ref, freq_ref, cos_ref, sin_ref):
    pos = pos_ref[0].astype(F32)
    ang = freq_ref[...] * pos
    c, s = jnp.cos(ang), jnp.sin(ang)
    ts = pos.shape[1]
    pad = HEAD_SLAB - QK_NOPE_DIM - QK_ROPE_DIM
    cos_ref[0] = jnp.concatenate(
        [jnp.ones((QK_NOPE_DIM, ts), F32), c, c, jnp.zeros((pad, ts), F32)], axis=0).T
    sin_ref[0] = jnp.concatenate(
        [jnp.zeros((QK_NOPE_DIM, ts), F32), s, s, jnp.zeros((pad, ts), F32)], axis=0).T


def _rope_tables(positions):
    b, s = positions.shape
    freqs = ROPE_THETA ** (-jnp.arange(0, QK_ROPE_DIM, 2, dtype=F32) / QK_ROPE_DIM)
    ts = s
    out = jax.ShapeDtypeStruct((b, s, HEAD_SLAB), F32)
    return pl.pallas_call(
        _rope_kernel,
        grid=(b, s // ts),
        in_specs=[
            pl.BlockSpec((1, 1, ts), lambda i, j: (i, 0, j)),
            pl.BlockSpec((QK_ROPE_DIM // 2, 1), lambda i, j: (0, 0)),
        ],
        out_specs=[pl.BlockSpec((1, ts, HEAD_SLAB), lambda i, j: (i, j, 0))] * 2,
        out_shape=[out, out],
        compiler_params=pltpu.CompilerParams(
            dimension_semantics=("arbitrary", "arbitrary")),
        name="rope_tables",
    )(positions.reshape(b, 1, s), freqs.reshape(-1, 1))


def _mix_in_kernel(x_ref, mod_ref, cos_ref, sin_ref, g_mix_ref, w_uvt_in_ref, w_in_ref, wst_ref,
                   bias_ref, g_q_ref, g_kv_ref, w_uq_ref, w_uk_ref, w_uvt_ref, v_one_ref,
                   g_og_ref, yg_ref, q_ref, k_ref, vt_ref, *, d_gmlp, q_rank, kv_rank, q_scale, tm_sub):
    nt = (((1,), (1,)), ((), ()))
    mod = mod_ref[...]
    shift1, scale1 = mod[0:1], mod[1:2]
    gain1 = g_mix_ref[...] * (1.0 + scale1)
    n_chunk = tm_sub // CHUNK
    dg = d_gmlp // GMLP_GROUPS
    o_kv = q_rank
    o_kr = o_kv + kv_rank
    shift = HEAD_SLAB - QK_ROPE_DIM

    row = lax.broadcasted_iota(jnp.int32, (CHUNK, CHUNK), 0)
    col = lax.broadcasted_iota(jnp.int32, (CHUNK, CHUNK), 1)
    w_mix = [jnp.where(row <= col, wst_ref[g], 0.0).astype(BF16) for g in range(GMLP_GROUPS)]
    bias = [jnp.concatenate([bias_ref[g]] * n_chunk, axis=1) for g in range(GMLP_GROUPS)]

    n_sub = x_ref.shape[1] // tm_sub
    proj = []
    for sb in range(n_sub):
        rows = slice(sb * tm_sub, (sb + 1) * tm_sub)
        h = (_rms(x_ref[0, rows, :]) * gain1 + shift1).astype(BF16)
        proj.append((_dot(h, w_in_ref[...]),
                     lax.dot_general(w_uvt_in_ref[...], h, nt, preferred_element_type=F32)))

    def mla(sb):
        rows = slice(sb * tm_sub, (sb + 1) * tm_sub)
        z = proj[sb][0]
        cos_t = cos_ref[0, rows, :]
        sin_t = sin_ref[0, rows, :]
        c_q = (_rms(z[:, :o_kv]) * g_q_ref[...]).astype(BF16)
        qf = _dot(c_q, w_uq_ref[...])
        cos_q = cos_t * q_scale
        sin_q = sin_t * q_scale
        for hd in range(MLA_HEADS):
            qs = qf[:, hd * HEAD_SLAB:(hd + 1) * HEAD_SLAB]
            q_ref[0, hd, rows, :] = (qs * cos_q + pltpu.roll(qs, shift, 1) * sin_q).astype(BF16)

        c_kv = (_rms(z[:, o_kv:o_kr]) * g_kv_ref[...]).astype(BF16)
        kf = _dot(c_kv, w_uk_ref[...])
        kr = z[:, o_kr:]
        kr = kr * cos_t + pltpu.roll(kr, shift, 1) * sin_t
        for hd in range(MLA_HEADS):
            k_ref[0, hd, rows, :] = (kf[:, hd * HEAD_SLAB:(hd + 1) * HEAD_SLAB] + kr).astype(BF16)
        vt = lax.dot_general(w_uvt_ref[...], c_kv, nt,
                             preferred_element_type=F32)
        vt_ref[0, :, rows] = (vt + v_one_ref[...]).astype(BF16)

    def gmlp(sb):
        rows = slice(sb * tm_sub, (sb + 1) * tm_sub)
        uv_t = proj[sb][1]
        gu = _gelu_tanh(uv_t[:d_gmlp]).reshape(GMLP_GROUPS, dg, tm_sub)
        gv = _gelu_tanh(uv_t[d_gmlp:]).reshape(GMLP_GROUPS, dg, tm_sub)
        dv = gv - jnp.mean(gv, axis=1, keepdims=True)
        vn = (dv * lax.rsqrt(jnp.mean(dv * dv, axis=1, keepdims=True) + EPS)).astype(BF16)
        mixed = []
        for g in range(GMLP_GROUPS):
            lhs = jnp.concatenate([vn[g, :, c * CHUNK:(c + 1) * CHUNK] for c in range(n_chunk)], axis=0)
            out = _dot(lhs, w_mix[g])
            out = jnp.concatenate([out[c * dg:(c + 1) * dg] for c in range(n_chunk)], axis=1)
            mixed.append(out + bias[g])
        yg_t = (gu * jnp.stack(mixed)).reshape(d_gmlp, tm_sub)
        yg_t = yg_t * lax.rsqrt(jnp.mean(yg_t * yg_t, axis=0, keepdims=True) + EPS) * g_og_ref[...]
        yg_ref[0, rows, :] = yg_t.T.astype(BF16)

    for sb in range(n_sub):
        gmlp(sb)
    for sb in range(n_sub):
        mla(sb)


def _layer_spec(a, layer, **kw):
    return pl.BlockSpec((None,) + a.shape[1:], lambda i, j: (layer,) + (0,) * (a.ndim - 1), **kw)


def _mod_spec(mod, layer):
    return pl.BlockSpec((None, None) + mod.shape[2:], lambda i, j: (layer, i, 0, 0))


def _mix_in(layer, x, mod, cos_t, sin_t, g_mix, w_uvt_in, w_in, wst, bias, g_q, g_kv, w_uq, w_uk,
            w_uvt, v_one, g_og):
    b, s, d = x.shape
    d_gmlp = g_og.shape[1]
    q_rank = g_q.shape[-1]
    kv_rank = g_kv.shape[-1]
    tm = TM_IN
    q_scale = (QK_NOPE_DIM + QK_ROPE_DIM) ** -0.5 * math.log2(math.e)

    def par(a):
        return _layer_spec(a, layer)

    kern = functools.partial(_mix_in_kernel, d_gmlp=d_gmlp, q_rank=q_rank,
                             kv_rank=kv_rank, q_scale=q_scale, tm_sub=TM_SUB)
    slab = jax.ShapeDtypeStruct((b, MLA_HEADS, s, HEAD_SLAB), BF16)
    return pl.pallas_call(
        kern,
        grid=(b, s // tm),
        in_specs=[
            pl.BlockSpec((1, tm, d), lambda i, j: (i, j, 0)),
            _mod_spec(mod, layer),
            pl.BlockSpec((1, tm, HEAD_SLAB), lambda i, j: (i, j, 0)),
            pl.BlockSpec((1, tm, HEAD_SLAB), lambda i, j: (i, j, 0)),
            par(g_mix), par(w_uvt_in), par(w_in), par(wst), par(bias),
            par(g_q), par(g_kv), par(w_uq), par(w_uk), par(w_uvt),
            pl.BlockSpec(v_one.shape, lambda i, j: (0, 0)), par(g_og),
        ],
        out_specs=[
            pl.BlockSpec((1, tm, d_gmlp), lambda i, j: (i, j, 0)),
            pl.BlockSpec((1, MLA_HEADS, tm, HEAD_SLAB), lambda i, j: (i, 0, j, 0)),
            pl.BlockSpec((1, MLA_HEADS, tm, HEAD_SLAB), lambda i, j: (i, 0, j, 0)),
            pl.BlockSpec((1, MLA_HEADS * V_SLAB, tm), lambda i, j: (i, 0, j)),
        ],
        out_shape=[
            jax.ShapeDtypeStruct((b, s, d_gmlp), BF16),
            slab, slab,
            jax.ShapeDtypeStruct((b, MLA_HEADS * V_SLAB, s), BF16),
        ],
        compiler_params=pltpu.CompilerParams(
            dimension_semantics=("arbitrary", "arbitrary"),
            vmem_limit_bytes=VMEM_LIMIT_BYTES),
        name="mix_in",
    )(x, mod, cos_t, sin_t, g_mix, w_uvt_in, w_in, wst, bias, g_q, g_kv, w_uq, w_uk, w_uvt, v_one,
      g_og)


def _attn_kernel(q_ref, k_ref, vt_ref, o_ref, *, tq):
    n_pair = q_ref.shape[1] // 2
    s_len = q_ref.shape[2]
    nt = (((1,), (1,)), ((), ()))
    half = tq // 2
    key = lax.broadcasted_iota(jnp.int32, (half, half), 0)
    qry = lax.broadcasted_iota(jnp.int32, (half, half), 1)
    causal = key <= qry
    n_blk = s_len // tq

    def qk(pair, c):
        q0, kmid, kend = c * tq, c * tq + half, (c + 1) * tq
        return [(lax.dot_general(k_ref[0, hd, :kmid, :], q_ref[0, hd, q0:kend, :], nt,
                                 preferred_element_type=F32),
                 lax.dot_general(k_ref[0, hd, kmid:kend, :], q_ref[0, hd, kmid:kend, :], nt,
                                 preferred_element_type=F32))
                for hd in (2 * pair, 2 * pair + 1)]

    jobs = [(pair, c) for pair in range(n_pair)
            for c in (range(n_blk) if pair % 2 == 0 else range(n_blk - 1, -1, -1))]
    nxt = qk(*jobs[0])
    for i, (pair, c) in enumerate(jobs):
        q0, kmid, kend = c * tq, c * tq + half, (c + 1) * tq
        scores = nxt
        if i + 1 < len(jobs):
            nxt = qk(*jobs[i + 1])
        halves = []
        for e in range(2):
            a, b = scores[e]
            top = jnp.concatenate([jnp.where(causal, a[q0:, :half], -1e30), a[q0:, half:]], axis=1)
            a = top if c == 0 else jnp.concatenate([a[:q0], top], axis=0)
            b = jnp.where(causal, b, -1e30)
            m = jnp.max(a, axis=0, keepdims=True)
            m = jnp.concatenate(
                [m[:, :half], jnp.maximum(m[:, half:], jnp.max(b, axis=0, keepdims=True))], axis=1)
            p_a = jnp.exp2(a - m).astype(BF16)
            p_b = jnp.exp2(b - m[:, half:]).astype(BF16)
            hd = 2 * pair + e
            vt = vt_ref.at[0, hd * V_SLAB:(hd + 1) * V_SLAB, :]
            o = _dot(vt[:, :kmid], p_a)
            o = jnp.concatenate([o[:, :half], o[:, half:] + _dot(vt[:, kmid:kend], p_b)], axis=1)
            halves.append(o[:V_HEAD_DIM] * (1.0 / o[V_HEAD_DIM:V_HEAD_DIM + 1]))
        o_ref[0, q0:kend, pair * LANES:(pair + 1) * LANES] = (
            jnp.concatenate(halves, axis=0).T.astype(o_ref.dtype))


def _attention(q, k, vt):
    b, hds, s, _ = q.shape
    hpb = ATTN_HEADS_PER_STEP
    return pl.pallas_call(
        functools.partial(_attn_kernel, tq=TQ),
        grid=(b, hds // hpb),
        in_specs=[
            pl.BlockSpec((1, hpb, s, HEAD_SLAB), lambda bi, j: (bi, j, 0, 0)),
            pl.BlockSpec((1, hpb, s, HEAD_SLAB), lambda bi, j: (bi, j, 0, 0)),
            pl.BlockSpec((1, hpb * V_SLAB, s), lambda bi, j: (bi, j, 0)),
        ],
        out_specs=pl.BlockSpec((1, s, hpb * V_HEAD_DIM), lambda bi, j: (bi, 0, j)),
        out_shape=jax.ShapeDtypeStruct((b, s, hds * V_HEAD_DIM), BF16),
        compiler_params=pltpu.CompilerParams(
            dimension_semantics=("arbitrary", "arbitrary"),
            vmem_limit_bytes=VMEM_LIMIT_BYTES),
        name="attn",
    )(q, k, vt)


def _mix_out_kernel(x_ref, yg_ref, ya_ref, mod_ref, g_oa_ref, w_out_ref, g_ffn_ref,
                    w1_ref, w2_ref, g_fin_ref, o_ref, *, final, tm_sub):
    mod = mod_ref[...]
    gate1, shift2, scale2, gate2 = mod[2:3], mod[3:4], mod[4:5], mod[5:6]
    gain2 = g_ffn_ref[...] * (1.0 + scale2)
    d_g = yg_ref.shape[-1]
    n_sub = x_ref.shape[1] // tm_sub

    mids = []
    for sb in range(n_sub):
        rows = slice(sb * tm_sub, (sb + 1) * tm_sub)
        ya = (_rms(ya_ref[0, rows, :].astype(F32)) * g_oa_ref[...]).astype(BF16)
        mixed = _dot(yg_ref[0, rows, :], w_out_ref[:d_g, :]) + _dot(ya, w_out_ref[d_g:, :])
        x1 = x_ref[0, rows, :] + gate1 * mixed
        mids.append((x1, (_rms(x1) * gain2 + shift2).astype(BF16)))

    for sb in range(n_sub):
        rows = slice(sb * tm_sub, (sb + 1) * tm_sub)
        x1, h = mids[sb]
        f = jnp.zeros_like(x1)
        for c in range(w1_ref.shape[1] // FF_CHUNK):
            a = jnp.maximum(_dot(h, w1_ref[:, c * FF_CHUNK:(c + 1) * FF_CHUNK]), 0.0)
            f = f + _dot((a * a).astype(BF16), w2_ref[c * FF_CHUNK:(c + 1) * FF_CHUNK, :])
        x2 = x1 + gate2 * f
        if final:
            x2 = _rms(x2) * g_fin_ref[...]
        o_ref[0, rows, :] = x2


def _mix_out(layer, x, yg, ya, mod, g_oa, w_out, g_ffn, w1, w2, g_fin, final):
    b, s, d = x.shape
    tm = TM_OUT

    def par(a):
        return _layer_spec(a, layer, pipeline_mode=pl.Buffered(1))

    def tok(a):
        return pl.BlockSpec((1, tm, a.shape[-1]), lambda i, j: (i, j, 0))

    return pl.pallas_call(
        functools.partial(_mix_out_kernel, final=final, tm_sub=TM_SUB),
        grid=(b, s // tm),
        in_specs=[
            tok(x), tok(yg), tok(ya),
            _mod_spec(mod, layer),
            par(g_oa), par(w_out), par(g_ffn), par(w1), par(w2),
            pl.BlockSpec(g_fin.shape, lambda i, j: (0, 0), pipeline_mode=pl.Buffered(1)),
        ],
        out_specs=tok(x),
        out_shape=jax.ShapeDtypeStruct(x.shape, F32),
        compiler_params=pltpu.CompilerParams(
            dimension_semantics=("arbitrary", "arbitrary"),
            vmem_limit_bytes=VMEM_LIMIT_BYTES),
        name="mix_out_final" if final else "mix_out",
    )(x, yg, ya, mod, g_oa, w_out, g_ffn, w1, w2, g_fin)


def _rotate_half_cols(w):
    half = w.shape[-1] // 2
    return jnp.concatenate([-w[..., half:], w[..., :half]], axis=-1)


def _prep_weights(w_in, mla_w_uq, mla_w_ukv, d_gmlp):
    depth, d, _ = w_in.shape
    q_rank = mla_w_uq.shape[1]
    kv_rank = mla_w_ukv.shape[1]
    o_kr = w_in.shape[-1] - QK_ROPE_DIM
    kr = w_in[..., o_kr:]
    w_uvt_in = jnp.swapaxes(w_in[..., :2 * d_gmlp], 1, 2).astype(BF16)
    w_in_ext = jnp.concatenate(
        [w_in[..., 2 * d_gmlp:o_kr], jnp.zeros((depth, d, QK_NOPE_DIM), F32), kr,
         _rotate_half_cols(kr)], axis=-1).astype(BF16)

    wq = mla_w_uq.reshape(depth, q_rank, MLA_HEADS, QK_NOPE_DIM + QK_ROPE_DIM)
    rope = wq[..., QK_NOPE_DIM:]
    w_uq_ext = jnp.concatenate([wq, _rotate_half_cols(rope)], axis=-1)
    w_uq_ext = w_uq_ext.reshape(depth, q_rank, MLA_HEADS * HEAD_SLAB).astype(BF16)

    wkv = mla_w_ukv.reshape(depth, kv_rank, MLA_HEADS, QK_NOPE_DIM + V_HEAD_DIM)
    w_k = jnp.concatenate(
        [wkv[..., :QK_NOPE_DIM],
         jnp.zeros((depth, kv_rank, MLA_HEADS, HEAD_SLAB - QK_NOPE_DIM), F32)], axis=-1)
    w_uk_ext = w_k.reshape(depth, kv_rank, MLA_HEADS * HEAD_SLAB).astype(BF16)
    w_v = jnp.concatenate(
        [wkv[..., QK_NOPE_DIM:], jnp.zeros((depth, kv_rank, MLA_HEADS, BF16_ROWS), F32)], axis=-1)
    w_uvt = jnp.swapaxes(w_v.reshape(depth, kv_rank, MLA_HEADS * V_SLAB), 1, 2).astype(BF16)
    v_one = (jnp.arange(MLA_HEADS * V_SLAB) % V_SLAB == V_HEAD_DIM).astype(F32).reshape(-1, 1)
    return w_uvt_in, w_in_ext, w_uq_ext, w_uk_ext, w_uvt, v_one


def kernel(x, c, positions, w_ada, b_ada, norm_mix_g, w_in, gmlp_ws, gmlp_bs, mla_q_norm_g,
           mla_kv_norm_g, mla_w_uq, mla_w_ukv, out_norm_gmlp_g, out_norm_mla_g, w_out,
           norm_ffn_g, w_ff1, w_ff2, final_norm_g):
    b, s, d = x.shape
    depth = w_ada.shape[0]
    d_gmlp = out_norm_gmlp_g.shape[-1]

    mod = _adaln(c, w_ada, b_ada).reshape(depth, b, N_MOD, d)
    cos_t, sin_t = _rope_tables(positions)
    w_uvt_in, w_in_ext, w_uq_ext, w_uk_ext, w_uvt, v_one = _prep_weights(
        w_in, mla_w_uq, mla_w_ukv, d_gmlp)
    wst = jnp.swapaxes(gmlp_ws, 2, 3)
    bias = gmlp_bs[:, :, None, :]
    g_og = out_norm_gmlp_g[:, :, None]
    w_out_b = w_out.astype(BF16)
    w1_b = w_ff1.astype(BF16)
    w2_b = w_ff2.astype(BF16)
    g_fin = final_norm_g.reshape(1, d)

    def rows(g):
        return g[:, None, :]

    for l in range(depth):
        yg, q, k, vt = _mix_in(
            l, x, mod, cos_t, sin_t, rows(norm_mix_g), w_uvt_in, w_in_ext, wst, bias,
            rows(mla_q_norm_g), rows(mla_kv_norm_g), w_uq_ext, w_uk_ext, w_uvt, v_one, g_og)
        ya = _attention(q, k, vt)
        x = _mix_out(l, x, yg, ya, mod, rows(out_norm_mla_g), w_out_b, rows(norm_ffn_g),
                     w1_b, w2_b, g_fin, final=(l == depth - 1))
    return x
```

```python
import functools
import math

import jax
import jax.numpy as jnp
from jax import lax
from jax.experimental import pallas as pl
from jax.experimental.pallas import tpu as pltpu

F32 = jnp.float32
BF16 = jnp.bfloat16

EPS = 1e-6
ROPE_THETA = 10000.0
N_MOD = 6

GMLP_GROUPS = 8
CHUNK = 128
MLA_HEADS = 8
QK_NOPE_DIM = 64
QK_ROPE_DIM = 32
V_HEAD_DIM = 64
HEAD_SLAB = 128
LANES = 128
BF16_ROWS = 16
V_SLAB = V_HEAD_DIM + BF16_ROWS

VMEM_LIMIT_BYTES = 56 * 1024 * 1024

TM_IN = 1024
TM_SUB = 512
TM_OUT = 1024
TQ = 512
ATTN_HEADS_PER_STEP = 4
FF_CHUNK = 1024
ADA_TN = 1536


def _dot(a, b):
    return jnp.dot(a, b, preferred_element_type=F32)


def _rms(x):
    return x * lax.rsqrt(jnp.mean(x * x, axis=-1, keepdims=True) + EPS)


def _gelu_tanh(x):
    c = math.sqrt(2.0 / math.pi)
    half = 0.5 * x
    return half + half * jnp.tanh(x * (c + (c * 0.044715) * (x * x)))


def _adaln_kernel(c_ref, w_ref, b_ref, o_ref):
    c = c_ref[...]
    c_act = (c / (1.0 + jnp.exp(-c))).astype(BF16)
    o_ref[0] = _dot(c_act, w_ref[0].astype(BF16)) + b_ref[0]


def _adaln(c, w_ada, b_ada):
    depth, d, n = w_ada.shape
    b = c.shape[0]
    return pl.pallas_call(
        _adaln_kernel,
        grid=(depth, n // ADA_TN),
        in_specs=[
            pl.BlockSpec((b, d), lambda l, j: (0, 0)),
            pl.BlockSpec((1, d, ADA_TN), lambda l, j: (l, 0, j)),
            pl.BlockSpec((1, 1, ADA_TN), lambda l, j: (l, 0, j)),
        ],
        out_specs=pl.BlockSpec((1, b, ADA_TN), lambda l, j: (l, 0, j)),
        out_shape=jax.ShapeDtypeStruct((depth, b, n), F32),
        compiler_params=pltpu.CompilerParams(
            dimension_semantics=("arbitrary", "arbitrary"),
            vmem_limit_bytes=VMEM_LIMIT_BYTES),
        name="adaln_mod",
    )(c, w_ada, b_ada.reshape(depth, 1, n))


def _rope_kernel(pos_ref, freq_ref, cos_ref, sin_ref):
    pos = pos_ref[0].astype(F32)
    ang = freq_ref[...] * pos
    c, s = jnp.cos(ang), jnp.sin(ang)
    ts = pos.shape[1]
    pad = HEAD_SLAB - QK_NOPE_DIM - QK_ROPE_DIM
    cos_ref[0] = jnp.concatenate(
        [jnp.ones((QK_NOPE_DIM, ts), F32), c, c, jnp.zeros((pad, ts), F32)], axis=0).T
    sin_ref[0] = jnp.concatenate(
        [jnp.zeros((QK_NOPE_DIM, ts), F32), s, s, jnp.zeros((pad, ts), F32)], axis=0).T


def _rope_tables(positions):
    b, s = positions.shape
    freqs = ROPE_THETA ** (-jnp.arange(0, QK_ROPE_DIM, 2, dtype=F32) / QK_ROPE_DIM)
    ts = s
    out = jax.ShapeDtypeStruct((b, s, HEAD_SLAB), F32)
    return pl.pallas_call(
        _rope_kernel,
        grid=(b, s // ts),
        in_specs=[
            pl.BlockSpec((1, 1, ts), lambda i, j: (i, 0, j)),
            pl.BlockSpec((QK_ROPE_DIM // 2, 1), lambda i, j: (0, 0)),
        ],
        out_specs=[pl.BlockSpec((1, ts, HEAD_SLAB), lambda i, j: (i, j, 0))] * 2,
        out_shape=[out, out],
        compiler_params=pltpu.CompilerParams(
            dimension_semantics=("arbitrary", "arbitrary")),
        name="rope_tables",
    )(positions.reshape(b, 1, s), freqs.reshape(-1, 1))


def _mix_in_kernel(x_ref, mod_ref, cos_ref, sin_ref, g_mix_ref, w_uvt_in_ref, w_in_ref, wst_ref,
                   bias_ref, g_q_ref, g_kv_ref, w_uq_ref, w_uk_ref, w_uvt_ref, v_one_ref,
                   g_og_ref, w_out_f_ref, w1_f_ref, w2_f_ref, yg_ref, q_ref, k_ref, vt_ref,
                   w_out_b_ref, w1_b_ref, w2_b_ref, *, d_gmlp, q_rank, kv_rank, q_scale, tm_sub):
    w_out_b_ref[...] = w_out_f_ref[...].astype(BF16)
    w1_b_ref[...] = w1_f_ref[...].astype(BF16)
    w2_b_ref[...] = w2_f_ref[...].astype(BF16)

    nt = (((1,), (1,)), ((), ()))
    mod = mod_ref[...]
    shift1, scale1 = mod[0:1], mod[1:2]
    gain1 = g_mix_ref[...] * (1.0 + scale1)
    n_chunk = tm_sub // CHUNK
    dg = d_gmlp // GMLP_GROUPS
    o_kv = q_rank
    o_kr = o_kv + kv_rank
    shift = HEAD_SLAB - QK_ROPE_DIM

    row = lax.broadcasted_iota(jnp.int32, (CHUNK, CHUNK), 0)
    col = lax.broadcasted_iota(jnp.int32, (CHUNK, CHUNK), 1)
    w_mix = [jnp.where(row <= col, wst_ref[g], 0.0).astype(BF16) for g in range(GMLP_GROUPS)]
    bias = [jnp.concatenate([bias_ref[g]] * n_chunk, axis=1) for g in range(GMLP_GROUPS)]

    n_sub = x_ref.shape[1] // tm_sub
    proj = []
    for sb in range(n_sub):
        rows = slice(sb * tm_sub, (sb + 1) * tm_sub)
        h = (_rms(x_ref[0, rows, :]) * gain1 + shift1).astype(BF16)
        proj.append((_dot(h, w_in_ref[...]),
                     lax.dot_general(w_uvt_in_ref[...], h, nt, preferred_element_type=F32)))

    def mla(sb):
        rows = slice(sb * tm_sub, (sb + 1) * tm_sub)
        z = proj[sb][0]
        cos_t = cos_ref[0, rows, :]
        sin_t = sin_ref[0, rows, :]
        c_q = (_rms(z[:, :o_kv]) * g_q_ref[...]).astype(BF16)
        qf = _dot(c_q, w_uq_ref[...])
        cos_q = cos_t * q_scale
        sin_q = sin_t * q_scale
        for hd in range(MLA_HEADS):
            qs = qf[:, hd * HEAD_SLAB:(hd + 1) * HEAD_SLAB]
            q_ref[0, hd, rows, :] = (qs * cos_q + pltpu.roll(qs, shift, 1) * sin_q).astype(BF16)

        c_kv = (_rms(z[:, o_kv:o_kr]) * g_kv_ref[...]).astype(BF16)
        kf = _dot(c_kv, w_uk_ref[...])
        kr = z[:, o_kr:]
        kr = kr * cos_t + pltpu.roll(kr, shift, 1) * sin_t
        for hd in range(MLA_HEADS):
            k_ref[0, hd, rows, :] = (kf[:, hd * HEAD_SLAB:(hd + 1) * HEAD_SLAB] + kr).astype(BF16)
        vt = lax.dot_general(w_uvt_ref[...], c_kv, nt,
                             preferred_element_type=F32)
        vt_ref[0, :, rows] = (vt + v_one_ref[...]).astype(BF16)

    def gmlp(sb):
        rows = slice(sb * tm_sub, (sb + 1) * tm_sub)
        uv_t = proj[sb][1]
        gu = _gelu_tanh(uv_t[:d_gmlp]).reshape(GMLP_GROUPS, dg, tm_sub)
        gv = _gelu_tanh(uv_t[d_gmlp:]).reshape(GMLP_GROUPS, dg, tm_sub)
        dv = gv - jnp.mean(gv, axis=1, keepdims=True)
        vn = (dv * lax.rsqrt(jnp.mean(dv * dv, axis=1, keepdims=True) + EPS)).astype(BF16)
        mixed = []
        for g in range(GMLP_GROUPS):
            lhs = jnp.concatenate([vn[g, :, c * CHUNK:(c + 1) * CHUNK] for c in range(n_chunk)], axis=0)
            out = _dot(lhs, w_mix[g])
            out = jnp.concatenate([out[c * dg:(c + 1) * dg] for c in range(n_chunk)], axis=1)
            mixed.append(out + bias[g])
        yg_t = (gu * jnp.stack(mixed)).reshape(d_gmlp, tm_sub)
        yg_t = yg_t * lax.rsqrt(jnp.mean(yg_t * yg_t, axis=0, keepdims=True) + EPS) * g_og_ref[...]
        yg_ref[0, rows, :] = yg_t.T.astype(BF16)

    for sb in range(n_sub):
        gmlp(sb)
    for sb in range(n_sub):
        mla(sb)


def _layer_spec(a, layer, **kw):
    return pl.BlockSpec((None,) + a.shape[1:], lambda i, j: (layer,) + (0,) * (a.ndim - 1), **kw)


def _mod_spec(mod, layer):
    return pl.BlockSpec((None, None) + mod.shape[2:], lambda i, j: (layer, i, 0, 0))


def _mix_in(layer, x, mod, cos_t, sin_t, g_mix, w_uvt_in, w_in, wst, bias, g_q, g_kv, w_uq, w_uk,
            w_uvt, v_one, g_og, w_out, w_ff1, w_ff2):
    b, s, d = x.shape
    n_j = s // TM_IN
    n_steps = b * n_j

    def slabbed(w):
        return w.reshape(w.shape[0], n_steps, w.shape[1] // n_steps, w.shape[2])

    def slab_in(w4):
        return pl.BlockSpec((None, None) + w4.shape[2:], lambda i, j: (layer, i * n_j + j, 0, 0))

    def slab_out(w4):
        return pl.BlockSpec((None,) + w4.shape[2:], lambda i, j: (i * n_j + j, 0, 0))

    f32_w = [slabbed(w) for w in (w_out, w_ff1, w_ff2)]
    d_gmlp = g_og.shape[1]
    q_rank = g_q.shape[-1]
    kv_rank = g_kv.shape[-1]
    tm = TM_IN
    q_scale = (QK_NOPE_DIM + QK_ROPE_DIM) ** -0.5 * math.log2(math.e)

    def par(a):
        return _layer_spec(a, layer)

    kern = functools.partial(_mix_in_kernel, d_gmlp=d_gmlp, q_rank=q_rank,
                             kv_rank=kv_rank, q_scale=q_scale, tm_sub=TM_SUB)
    slab = jax.ShapeDtypeStruct((b, MLA_HEADS, s, HEAD_SLAB), BF16)
    outs = pl.pallas_call(
        kern,
        grid=(b, n_j),
        in_specs=[
            pl.BlockSpec((1, tm, d), lambda i, j: (i, j, 0)),
            _mod_spec(mod, layer),
            pl.BlockSpec((1, tm, HEAD_SLAB), lambda i, j: (i, j, 0)),
            pl.BlockSpec((1, tm, HEAD_SLAB), lambda i, j: (i, j, 0)),
            par(g_mix), par(w_uvt_in), par(w_in), par(wst), par(bias),
            par(g_q), par(g_kv), par(w_uq), par(w_uk), par(w_uvt),
            pl.BlockSpec(v_one.shape, lambda i, j: (0, 0)), par(g_og),
        ] + [slab_in(w4) for w4 in f32_w],
        out_specs=[
            pl.BlockSpec((1, tm, d_gmlp), lambda i, j: (i, j, 0)),
            pl.BlockSpec((1, MLA_HEADS, tm, HEAD_SLAB), lambda i, j: (i, 0, j, 0)),
            pl.BlockSpec((1, MLA_HEADS, tm, HEAD_SLAB), lambda i, j: (i, 0, j, 0)),
            pl.BlockSpec((1, MLA_HEADS * V_SLAB, tm), lambda i, j: (i, 0, j)),
        ] + [slab_out(w4) for w4 in f32_w],
        out_shape=[
            jax.ShapeDtypeStruct((b, s, d_gmlp), BF16),
            slab, slab,
            jax.ShapeDtypeStruct((b, MLA_HEADS * V_SLAB, s), BF16),
        ] + [jax.ShapeDtypeStruct(w4.shape[1:], BF16) for w4 in f32_w],
        compiler_params=pltpu.CompilerParams(
            dimension_semantics=("arbitrary", "arbitrary"),
            vmem_limit_bytes=VMEM_LIMIT_BYTES),
        name="mix_in",
    )(x, mod, cos_t, sin_t, g_mix, w_uvt_in, w_in, wst, bias, g_q, g_kv, w_uq, w_uk, w_uvt, v_one,
      g_og, *f32_w)
    yg, q, k, vt = outs[:4]
    w_out_b, w1_b, w2_b = [o.reshape(w.shape[1:]) for o, w in zip(outs[4:], (w_out, w_ff1, w_ff2))]
    return yg, q, k, vt, w_out_b, w1_b, w2_b


def _attn_kernel(q_ref, k_ref, vt_ref, o_ref, *, tq):
    n_pair = q_ref.shape[1] // 2
    s_len = q_ref.shape[2]
    nt = (((1,), (1,)), ((), ()))
    half = tq // 2
    key = lax.broadcasted_iota(jnp.int32, (half, half), 0)
    qry = lax.broadcasted_iota(jnp.int32, (half, half), 1)
    causal = key <= qry
    n_blk = s_len // tq

    def qk(pair, c):
        q0, kmid, kend = c * tq, c * tq + half, (c + 1) * tq
        return [(lax.dot_general(k_ref[0, hd, :kmid, :], q_ref[0, hd, q0:kend, :], nt,
                                 preferred_element_type=F32),
                 lax.dot_general(k_ref[0, hd, kmid:kend, :], q_ref[0, hd, kmid:kend, :], nt,
                                 preferred_element_type=F32))
                for hd in (2 * pair, 2 * pair + 1)]

    jobs = [(pair, c) for pair in range(n_pair)
            for c in (range(n_blk) if pair % 2 == 0 else range(n_blk - 1, -1, -1))]
    nxt = qk(*jobs[0])
    for i, (pair, c) in enumerate(jobs):
        q0, kmid, kend = c * tq, c * tq + half, (c + 1) * tq
        scores = nxt
        if i + 1 < len(jobs):
            nxt = qk(*jobs[i + 1])
        halves = []
        for e in range(2):
            a, b = scores[e]
            top = jnp.concatenate([jnp.where(causal, a[q0:, :half], -1e30), a[q0:, half:]], axis=1)
            a = top if c == 0 else jnp.concatenate([a[:q0], top], axis=0)
            b = jnp.where(causal, b, -1e30)
            m = jnp.max(a, axis=0, keepdims=True)
            m = jnp.concatenate(
                [m[:, :half], jnp.maximum(m[:, half:], jnp.max(b, axis=0, keepdims=True))], axis=1)
            p_a = jnp.exp2(a - m).astype(BF16)
            p_b = jnp.exp2(b - m[:, half:]).astype(BF16)
            hd = 2 * pair + e
            vt = vt_ref.at[0, hd * V_SLAB:(hd + 1) * V_SLAB, :]
            o = _dot(vt[:, :kmid], p_a)
            o = jnp.concatenate([o[:, :half], o[:, half:] + _dot(vt[:, kmid:kend], p_b)], axis=1)
            halves.append(o[:V_HEAD_DIM] * (1.0 / o[V_HEAD_DIM:V_HEAD_DIM + 1]))
        o_ref[0, q0:kend, pair * LANES:(pair + 1) * LANES] = (
            jnp.concatenate(halves, axis=0).T.astype(o_ref.dtype))


def _attention(q, k, vt):
    b, hds, s, _ = q.shape
    hpb = ATTN_HEADS_PER_STEP
    return pl.pallas_call(
        functools.partial(_attn_kernel, tq=TQ),
        grid=(b, hds // hpb),
        in_specs=[
            pl.BlockSpec((1, hpb, s, HEAD_SLAB), lambda bi, j: (bi, j, 0, 0)),
            pl.BlockSpec((1, hpb, s, HEAD_SLAB), lambda bi, j: (bi, j, 0, 0)),
            pl.BlockSpec((1, hpb * V_SLAB, s), lambda bi, j: (bi, j, 0)),
        ],
        out_specs=pl.BlockSpec((1, s, hpb * V_HEAD_DIM), lambda bi, j: (bi, 0, j)),
        out_shape=jax.ShapeDtypeStruct((b, s, hds * V_HEAD_DIM), BF16),
        compiler_params=pltpu.CompilerParams(
            dimension_semantics=("arbitrary", "arbitrary"),
            vmem_limit_bytes=VMEM_LIMIT_BYTES),
        name="attn",
    )(q, k, vt)


def _mix_out_kernel(x_ref, yg_ref, ya_ref, mod_ref, g_oa_ref, w_out_ref, g_ffn_ref,
                    w1_ref, w2_ref, g_fin_ref, o_ref, *, final, tm_sub):
    mod = mod_ref[...]
    gate1, shift2, scale2, gate2 = mod[2:3], mod[3:4], mod[4:5], mod[5:6]
    gain2 = g_ffn_ref[...] * (1.0 + scale2)
    d_g = yg_ref.shape[-1]
    n_sub = x_ref.shape[1] // tm_sub

    mids = []
    for sb in range(n_sub):
        rows = slice(sb * tm_sub, (sb + 1) * tm_sub)
        ya = (_rms(ya_ref[0, rows, :].astype(F32)) * g_oa_ref[...]).astype(BF16)
        mixed = _dot(yg_ref[0, rows, :], w_out_ref[:d_g, :]) + _dot(ya, w_out_ref[d_g:, :])
        x1 = x_ref[0, rows, :] + gate1 * mixed
        mids.append((x1, (_rms(x1) * gain2 + shift2).astype(BF16)))

    for sb in range(n_sub):
        rows = slice(sb * tm_sub, (sb + 1) * tm_sub)
        x1, h = mids[sb]
        f = jnp.zeros_like(x1)
        for c in range(w1_ref.shape[1] // FF_CHUNK):
            a = jnp.maximum(_dot(h, w1_ref[:, c * FF_CHUNK:(c + 1) * FF_CHUNK]), 0.0)
            f = f + _dot((a * a).astype(BF16), w2_ref[c * FF_CHUNK:(c + 1) * FF_CHUNK, :])
        x2 = x1 + gate2 * f
        if final:
            x2 = _rms(x2) * g_fin_ref[...]
        o_ref[0, rows, :] = x2


def _mix_out(layer, x, yg, ya, mod, g_oa, w_out, g_ffn, w1, w2, g_fin, final):
    b, s, d = x.shape
    tm = TM_OUT

    def par(a):
        return _layer_spec(a, layer, pipeline_mode=pl.Buffered(1))

    def whole(a):
        return pl.BlockSpec(a.shape, lambda i, j: (0,) * a.ndim, pipeline_mode=pl.Buffered(1))

    def tok(a):
        return pl.BlockSpec((1, tm, a.shape[-1]), lambda i, j: (i, j, 0))

    return pl.pallas_call(
        functools.partial(_mix_out_kernel, final=final, tm_sub=TM_SUB),
        grid=(b, s // tm),
        in_specs=[
            tok(x), tok(yg), tok(ya),
            _mod_spec(mod, layer),
            par(g_oa), whole(w_out), par(g_ffn), whole(w1), whole(w2), whole(g_fin),
        ],
        out_specs=tok(x),
        out_shape=jax.ShapeDtypeStruct(x.shape, F32),
        compiler_params=pltpu.CompilerParams(
            dimension_semantics=("arbitrary", "arbitrary"),
            vmem_limit_bytes=VMEM_LIMIT_BYTES),
        name="mix_out_final" if final else "mix_out",
    )(x, yg, ya, mod, g_oa, w_out, g_ffn, w1, w2, g_fin)


def _rotate_half_cols(w):
    half = w.shape[-1] // 2
    return jnp.concatenate([-w[..., half:], w[..., :half]], axis=-1)


def _prep_weights(w_in, mla_w_uq, mla_w_ukv, d_gmlp):
    depth, d, _ = w_in.shape
    q_rank = mla_w_uq.shape[1]
    kv_rank = mla_w_ukv.shape[1]
    o_kr = w_in.shape[-1] - QK_ROPE_DIM
    kr = w_in[..., o_kr:]
    w_uvt_in = jnp.swapaxes(w_in[..., :2 * d_gmlp], 1, 2).astype(BF16)
    w_in_ext = jnp.concatenate(
        [w_in[..., 2 * d_gmlp:o_kr], jnp.zeros((depth, d, QK_NOPE_DIM), F32), kr,
         _rotate_half_cols(kr)], axis=-1).astype(BF16)

    wq = mla_w_uq.reshape(depth, q_rank, MLA_HEADS, QK_NOPE_DIM + QK_ROPE_DIM)
    rope = wq[..., QK_NOPE_DIM:]
    w_uq_ext = jnp.concatenate([wq, _rotate_half_cols(rope)], axis=-1)
    w_uq_ext = w_uq_ext.reshape(depth, q_rank, MLA_HEADS * HEAD_SLAB).astype(BF16)

    wkv = mla_w_ukv.reshape(depth, kv_rank, MLA_HEADS, QK_NOPE_DIM + V_HEAD_DIM)
    w_k = jnp.concatenate(
        [wkv[..., :QK_NOPE_DIM],
         jnp.zeros((depth, kv_rank, MLA_HEADS, HEAD_SLAB - QK_NOPE_DIM), F32)], axis=-1)
    w_uk_ext = w_k.reshape(depth, kv_rank, MLA_HEADS * HEAD_SLAB).astype(BF16)
    w_v = jnp.concatenate(
        [wkv[..., QK_NOPE_DIM:], jnp.zeros((depth, kv_rank, MLA_HEADS, BF16_ROWS), F32)], axis=-1)
    w_uvt = jnp.swapaxes(w_v.reshape(depth, kv_rank, MLA_HEADS * V_SLAB), 1, 2).astype(BF16)
    v_one = (jnp.arange(MLA_HEADS * V_SLAB) % V_SLAB == V_HEAD_DIM).astype(F32).reshape(-1, 1)
    return w_uvt_in, w_in_ext, w_uq_ext, w_uk_ext, w_uvt, v_one


def kernel(x, c, positions, w_ada, b_ada, norm_mix_g, w_in, gmlp_ws, gmlp_bs, mla_q_norm_g,
           mla_kv_norm_g, mla_w_uq, mla_w_ukv, out_norm_gmlp_g, out_norm_mla_g, w_out,
           norm_ffn_g, w_ff1, w_ff2, final_norm_g):
    b, s, d = x.shape
    depth = w_ada.shape[0]
    d_gmlp = out_norm_gmlp_g.shape[-1]

    mod = _adaln(c, w_ada, b_ada).reshape(depth, b, N_MOD, d)
    cos_t, sin_t = _rope_tables(positions)
    w_uvt_in, w_in_ext, w_uq_ext, w_uk_ext, w_uvt, v_one = _prep_weights(
        w_in, mla_w_uq, mla_w_ukv, d_gmlp)
    wst = jnp.swapaxes(gmlp_ws, 2, 3)
    bias = gmlp_bs[:, :, None, :]
    g_og = out_norm_gmlp_g[:, :, None]
    g_fin = final_norm_g.reshape(1, d)

    def rows(g):
        return g[:, None, :]

    for l in range(depth):
        yg, q, k, vt, w_out_b, w1_b, w2_b = _mix_in(
            l, x, mod, cos_t, sin_t, rows(norm_mix_g), w_uvt_in, w_in_ext, wst, bias,
            rows(mla_q_norm_g), rows(mla_kv_norm_g), w_uq_ext, w_uk_ext, w_uvt, v_one, g_og,
            w_out, w_ff1, w_ff2)
        ya = _attention(q, k, vt)
        x = _mix_out(l, x, yg, ya, mod, rows(out_norm_mla_g), w_out_b, rows(norm_ffn_g),
                     w1_b, w2_b, g_fin, final=(l == depth - 1))
    return x
```

```python
import functools
import math

import jax
import jax.numpy as jnp
from jax import lax
from jax.experimental import pallas as pl
from jax.experimental.pallas import tpu as pltpu

F32 = jnp.float32
BF16 = jnp.bfloat16

EPS = 1e-6
ROPE_THETA = 10000.0
N_MOD = 6

GMLP_GROUPS = 8
CHUNK = 128
MLA_HEADS = 8
QK_NOPE_DIM = 64
QK_ROPE_DIM = 32
V_HEAD_DIM = 64
HEAD_SLAB = 128
LANES = 128
BF16_ROWS = 16
V_SLAB = V_HEAD_DIM + BF16_ROWS

VMEM_LIMIT_BYTES = 56 * 1024 * 1024

TM_IN = 1024
TM_SUB = 512
TM_OUT = 1024
TQ = 512
ATTN_HEADS_PER_STEP = 8
FF_CHUNK = 1024
ADA_TN = 1536


def _dot(a, b):
    return jnp.dot(a, b, preferred_element_type=F32)


def _rms(x):
    return x * lax.rsqrt(jnp.mean(x * x, axis=-1, keepdims=True) + EPS)


def _gelu_tanh(x):
    c = math.sqrt(2.0 / math.pi)
    half = 0.5 * x
    return half + half * jnp.tanh(x * (c + (c * 0.044715) * (x * x)))


def _adaln_kernel(c_ref, w_ref, b_ref, o_ref):
    c = c_ref[...]
    c_act = (c / (1.0 + jnp.exp(-c))).astype(BF16)
    o_ref[0] = _dot(c_act, w_ref[0].astype(BF16)) + b_ref[0]


def _adaln(c, w_ada, b_ada):
    depth, d, n = w_ada.shape
    b = c.shape[0]
    return pl.pallas_call(
        _adaln_kernel,
        grid=(depth, n // ADA_TN),
        in_specs=[
            pl.BlockSpec((b, d), lambda l, j: (0, 0)),
            pl.BlockSpec((1, d, ADA_TN), lambda l, j: (l, 0, j)),
            pl.BlockSpec((1, 1, ADA_TN), lambda l, j: (l, 0, j)),
        ],
        out_specs=pl.BlockSpec((1, b, ADA_TN), lambda l, j: (l, 0, j)),
        out_shape=jax.ShapeDtypeStruct((depth, b, n), F32),
        compiler_params=pltpu.CompilerParams(
            dimension_semantics=("arbitrary", "arbitrary"),
            vmem_limit_bytes=VMEM_LIMIT_BYTES),
        name="adaln_mod",
    )(c, w_ada, b_ada.reshape(depth, 1, n))


def _rope_kernel(pos_ref, freq_ref, cos_ref, sin_ref):
    pos = pos_ref[0].astype(F32)
    ang = freq_ref[...] * pos
    c, s = jnp.cos(ang), jnp.sin(ang)
    ts = pos.shape[1]
    pad = HEAD_SLAB - QK_NOPE_DIM - QK_ROPE_DIM
    cos_ref[0] = jnp.concatenate(
        [jnp.ones((QK_NOPE_DIM, ts), F32), c, c, jnp.zeros((pad, ts), F32)], axis=0).T
    sin_ref[0] = jnp.concatenate(
        [jnp.zeros((QK_NOPE_DIM, ts), F32), s, s, jnp.zeros((pad, ts), F32)], axis=0).T


def _rope_tables(positions):
    b, s = positions.shape
    freqs = ROPE_THETA ** (-jnp.arange(0, QK_ROPE_DIM, 2, dtype=F32) / QK_ROPE_DIM)
    ts = s
    out = jax.ShapeDtypeStruct((b, s, HEAD_SLAB), F32)
    return pl.pallas_call(
        _rope_kernel,
        grid=(b, s // ts),
        in_specs=[
            pl.BlockSpec((1, 1, ts), lambda i, j: (i, 0, j)),
            pl.BlockSpec((QK_ROPE_DIM // 2, 1), lambda i, j: (0, 0)),
        ],
        out_specs=[pl.BlockSpec((1, ts, HEAD_SLAB), lambda i, j: (i, j, 0))] * 2,
        out_shape=[out, out],
        compiler_params=pltpu.CompilerParams(
            dimension_semantics=("arbitrary", "arbitrary")),
        name="rope_tables",
    )(positions.reshape(b, 1, s), freqs.reshape(-1, 1))


def _mix_in_kernel(x_ref, mod_ref, cos_ref, sin_ref, g_mix_ref, w_uvt_in_ref, w_in_ref, wst_ref,
                   bias_ref, g_q_ref, g_kv_ref, w_uq_ref, w_uk_ref, w_uvt_ref, v_one_ref,
                   g_og_ref, w_out_f_ref, w1_f_ref, w2_f_ref, yg_ref, q_ref, k_ref, vt_ref,
                   w_out_b_ref, w1_b_ref, w2_b_ref, *, d_gmlp, q_rank, kv_rank, q_scale, tm_sub):
    w_out_b_ref[...] = w_out_f_ref[...].astype(BF16)
    w1_b_ref[...] = w1_f_ref[...].astype(BF16)
    w2_b_ref[...] = w2_f_ref[...].astype(BF16)

    nt = (((1,), (1,)), ((), ()))
    mod = mod_ref[...]
    shift1, scale1 = mod[0:1], mod[1:2]
    gain1 = g_mix_ref[...] * (1.0 + scale1)
    n_chunk = tm_sub // CHUNK
    dg = d_gmlp // GMLP_GROUPS
    o_kv = q_rank
    o_kr = o_kv + kv_rank
    shift = HEAD_SLAB - QK_ROPE_DIM

    row = lax.broadcasted_iota(jnp.int32, (CHUNK, CHUNK), 0)
    col = lax.broadcasted_iota(jnp.int32, (CHUNK, CHUNK), 1)
    w_mix = [jnp.where(row <= col, wst_ref[g], 0.0).astype(BF16) for g in range(GMLP_GROUPS)]
    bias = [jnp.concatenate([bias_ref[g]] * n_chunk, axis=1) for g in range(GMLP_GROUPS)]

    n_sub = x_ref.shape[1] // tm_sub
    proj = []
    for sb in range(n_sub):
        rows = slice(sb * tm_sub, (sb + 1) * tm_sub)
        h = (_rms(x_ref[0, rows, :]) * gain1 + shift1).astype(BF16)
        proj.append((_dot(h, w_in_ref[...]),
                     lax.dot_general(w_uvt_in_ref[...], h, nt, preferred_element_type=F32)))

    def mla(sb):
        rows = slice(sb * tm_sub, (sb + 1) * tm_sub)
        z = proj[sb][0]
        cos_t = cos_ref[0, rows, :]
        sin_t = sin_ref[0, rows, :]
        c_q = (_rms(z[:, :o_kv]) * g_q_ref[...]).astype(BF16)
        qf = _dot(c_q, w_uq_ref[...])
        cos_q = cos_t * q_scale
        sin_q = sin_t * q_scale
        for hd in range(MLA_HEADS):
            qs = qf[:, hd * HEAD_SLAB:(hd + 1) * HEAD_SLAB]
            q_ref[0, hd, rows, :] = (qs * cos_q + pltpu.roll(qs, shift, 1) * sin_q).astype(BF16)

        c_kv = (_rms(z[:, o_kv:o_kr]) * g_kv_ref[...]).astype(BF16)
        kf = _dot(c_kv, w_uk_ref[...])
        kr = z[:, o_kr:]
        kr = kr * cos_t + pltpu.roll(kr, shift, 1) * sin_t
        for hd in range(MLA_HEADS):
            k_ref[0, hd, rows, :] = (kf[:, hd * HEAD_SLAB:(hd + 1) * HEAD_SLAB] + kr).astype(BF16)
        vt = lax.dot_general(w_uvt_ref[...], c_kv, nt,
                             preferred_element_type=F32)
        vt_ref[0, :, rows] = (vt + v_one_ref[...]).astype(BF16)

    def gmlp(sb):
        rows = slice(sb * tm_sub, (sb + 1) * tm_sub)
        uv_t = proj[sb][1]
        gu = _gelu_tanh(uv_t[:d_gmlp]).reshape(GMLP_GROUPS, dg, tm_sub)
        gv = _gelu_tanh(uv_t[d_gmlp:]).reshape(GMLP_GROUPS, dg, tm_sub)
        dv = gv - jnp.mean(gv, axis=1, keepdims=True)
        vn = (dv * lax.rsqrt(jnp.mean(dv * dv, axis=1, keepdims=True) + EPS)).astype(BF16)
        mixed = []
        for g in range(GMLP_GROUPS):
            lhs = jnp.concatenate([vn[g, :, c * CHUNK:(c + 1) * CHUNK] for c in range(n_chunk)], axis=0)
            out = _dot(lhs, w_mix[g])
            out = jnp.concatenate([out[c * dg:(c + 1) * dg] for c in range(n_chunk)], axis=1)
            mixed.append(out + bias[g])
        yg_t = (gu * jnp.stack(mixed)).reshape(d_gmlp, tm_sub)
        yg_t = yg_t * lax.rsqrt(jnp.mean(yg_t * yg_t, axis=0, keepdims=True) + EPS) * g_og_ref[...]
        yg_ref[0, rows, :] = yg_t.T.astype(BF16)

    for sb in range(n_sub):
        gmlp(sb)
    for sb in range(n_sub):
        mla(sb)


def _layer_spec(a, layer, **kw):
    return pl.BlockSpec((None,) + a.shape[1:], lambda i, j: (layer,) + (0,) * (a.ndim - 1), **kw)


def _mod_spec(mod, layer):
    return pl.BlockSpec((None, None) + mod.shape[2:], lambda i, j: (layer, i, 0, 0))


def _mix_in(layer, x, mod, cos_t, sin_t, g_mix, w_uvt_in, w_in, wst, bias, g_q, g_kv, w_uq, w_uk,
            w_uvt, v_one, g_og, w_out, w_ff1, w_ff2):
    b, s, d = x.shape
    n_j = s // TM_IN
    n_steps = b * n_j

    def slabbed(w):
        return w.reshape(w.shape[0], n_steps, w.shape[1] // n_steps, w.shape[2])

    def slab_in(w4):
        return pl.BlockSpec((None, None) + w4.shape[2:], lambda i, j: (layer, i * n_j + j, 0, 0))

    def slab_out(w4):
        return pl.BlockSpec((None,) + w4.shape[2:], lambda i, j: (i * n_j + j, 0, 0))

    f32_w = [slabbed(w) for w in (w_out, w_ff1, w_ff2)]
    d_gmlp = g_og.shape[1]
    q_rank = g_q.shape[-1]
    kv_rank = g_kv.shape[-1]
    tm = TM_IN
    q_scale = (QK_NOPE_DIM + QK_ROPE_DIM) ** -0.5 * math.log2(math.e)

    def par(a):
        return _layer_spec(a, layer)

    kern = functools.partial(_mix_in_kernel, d_gmlp=d_gmlp, q_rank=q_rank,
                             kv_rank=kv_rank, q_scale=q_scale, tm_sub=TM_SUB)
    slab = jax.ShapeDtypeStruct((b, MLA_HEADS, s, HEAD_SLAB), BF16)
    outs = pl.pallas_call(
        kern,
        grid=(b, n_j),
        in_specs=[
            pl.BlockSpec((1, tm, d), lambda i, j: (i, j, 0)),
            _mod_spec(mod, layer),
            pl.BlockSpec((1, tm, HEAD_SLAB), lambda i, j: (i, j, 0)),
            pl.BlockSpec((1, tm, HEAD_SLAB), lambda i, j: (i, j, 0)),
            par(g_mix), par(w_uvt_in), par(w_in), par(wst), par(bias),
            par(g_q), par(g_kv), par(w_uq), par(w_uk), par(w_uvt),
            pl.BlockSpec(v_one.shape, lambda i, j: (0, 0)), par(g_og),
        ] + [slab_in(w4) for w4 in f32_w],
        out_specs=[
            pl.BlockSpec((1, tm, d_gmlp), lambda i, j: (i, j, 0)),
            pl.BlockSpec((1, MLA_HEADS, tm, HEAD_SLAB), lambda i, j: (i, 0, j, 0)),
            pl.BlockSpec((1, MLA_HEADS, tm, HEAD_SLAB), lambda i, j: (i, 0, j, 0)),
            pl.BlockSpec((1, MLA_HEADS * V_SLAB, tm), lambda i, j: (i, 0, j)),
        ] + [slab_out(w4) for w4 in f32_w],
        out_shape=[
            jax.ShapeDtypeStruct((b, s, d_gmlp), BF16),
            slab, slab,
            jax.ShapeDtypeStruct((b, MLA_HEADS * V_SLAB, s), BF16),
        ] + [jax.ShapeDtypeStruct(w4.shape[1:], BF16) for w4 in f32_w],
        compiler_params=pltpu.CompilerParams(
            dimension_semantics=("arbitrary", "arbitrary"),
            vmem_limit_bytes=VMEM_LIMIT_BYTES),
        name="mix_in",
    )(x, mod, cos_t, sin_t, g_mix, w_uvt_in, w_in, wst, bias, g_q, g_kv, w_uq, w_uk, w_uvt, v_one,
      g_og, *f32_w)
    yg, q, k, vt = outs[:4]
    w_out_b, w1_b, w2_b = [o.reshape(w.shape[1:]) for o, w in zip(outs[4:], (w_out, w_ff1, w_ff2))]
    return yg, q, k, vt, w_out_b, w1_b, w2_b


def _attn_kernel(q_ref, k_ref, vt_ref, o_ref, *, tq):
    n_pair = q_ref.shape[1] // 2
    s_len = q_ref.shape[2]
    nt = (((1,), (1,)), ((), ()))
    half = tq // 2
    key = lax.broadcasted_iota(jnp.int32, (half, half), 0)
    qry = lax.broadcasted_iota(jnp.int32, (half, half), 1)
    causal = key <= qry
    n_blk = s_len // tq

    def qk(pair, c):
        q0, kmid, kend = c * tq, c * tq + half, (c + 1) * tq
        return [(lax.dot_general(k_ref[0, hd, :kmid, :], q_ref[0, hd, q0:kend, :], nt,
                                 preferred_element_type=F32),
                 lax.dot_general(k_ref[0, hd, kmid:kend, :], q_ref[0, hd, kmid:kend, :], nt,
                                 preferred_element_type=F32))
                for hd in (2 * pair, 2 * pair + 1)]

    jobs = [(pair, c) for pair in range(n_pair)
            for c in (range(n_blk) if pair % 2 == 0 else range(n_blk - 1, -1, -1))]
    nxt = qk(*jobs[0])
    for i, (pair, c) in enumerate(jobs):
        q0, kmid, kend = c * tq, c * tq + half, (c + 1) * tq
        scores = nxt
        if i + 1 < len(jobs):
            nxt = qk(*jobs[i + 1])
        halves = []
        for e in range(2):
            a, b = scores[e]
            top = jnp.concatenate([jnp.where(causal, a[q0:, :half], -1e30), a[q0:, half:]], axis=1)
            a = top if c == 0 else jnp.concatenate([a[:q0], top], axis=0)
            b = jnp.where(causal, b, -1e30)
            m = jnp.max(a, axis=0, keepdims=True)
            m = jnp.concatenate(
                [m[:, :half], jnp.maximum(m[:, half:], jnp.max(b, axis=0, keepdims=True))], axis=1)
            p_a = jnp.exp2(a - m).astype(BF16)
            p_b = jnp.exp2(b - m[:, half:]).astype(BF16)
            hd = 2 * pair + e
            vt = vt_ref.at[0, hd * V_SLAB:(hd + 1) * V_SLAB, :]
            o = _dot(vt[:, :kmid], p_a)
            o = jnp.concatenate([o[:, :half], o[:, half:] + _dot(vt[:, kmid:kend], p_b)], axis=1)
            halves.append(o[:V_HEAD_DIM] * (1.0 / o[V_HEAD_DIM:V_HEAD_DIM + 1]))
        o_ref[0, q0:kend, pair * LANES:(pair + 1) * LANES] = (
            jnp.concatenate(halves, axis=0).T.astype(o_ref.dtype))


def _attention(q, k, vt):
    b, hds, s, _ = q.shape
    hpb = ATTN_HEADS_PER_STEP
    return pl.pallas_call(
        functools.partial(_attn_kernel, tq=TQ),
        grid=(b, hds // hpb),
        in_specs=[
            pl.BlockSpec((1, hpb, s, HEAD_SLAB), lambda bi, j: (bi, j, 0, 0)),
            pl.BlockSpec((1, hpb, s, HEAD_SLAB), lambda bi, j: (bi, j, 0, 0)),
            pl.BlockSpec((1, hpb * V_SLAB, s), lambda bi, j: (bi, j, 0)),
        ],
        out_specs=pl.BlockSpec((1, s, hpb * V_HEAD_DIM), lambda bi, j: (bi, 0, j)),
        out_shape=jax.ShapeDtypeStruct((b, s, hds * V_HEAD_DIM), BF16),
        compiler_params=pltpu.CompilerParams(
            dimension_semantics=("arbitrary", "arbitrary"),
            vmem_limit_bytes=VMEM_LIMIT_BYTES),
        name="attn",
    )(q, k, vt)


def _mix_out_kernel(x_ref, yg_ref, ya_ref, mod_ref, g_oa_ref, w_out_ref, g_ffn_ref,
                    w1_ref, w2_ref, g_fin_ref, o_ref, *, final, tm_sub):
    mod = mod_ref[...]
    gate1, shift2, scale2, gate2 = mod[2:3], mod[3:4], mod[4:5], mod[5:6]
    gain2 = g_ffn_ref[...] * (1.0 + scale2)
    d_g = yg_ref.shape[-1]
    n_sub = x_ref.shape[1] // tm_sub

    mids = []
    for sb in range(n_sub):
        rows = slice(sb * tm_sub, (sb + 1) * tm_sub)
        ya = (_rms(ya_ref[0, rows, :].astype(F32)) * g_oa_ref[...]).astype(BF16)
        mixed = _dot(yg_ref[0, rows, :], w_out_ref[:d_g, :]) + _dot(ya, w_out_ref[d_g:, :])
        x1 = x_ref[0, rows, :] + gate1 * mixed
        mids.append((x1, (_rms(x1) * gain2 + shift2).astype(BF16)))

    for sb in range(n_sub):
        rows = slice(sb * tm_sub, (sb + 1) * tm_sub)
        x1, h = mids[sb]
        f = jnp.zeros_like(x1)
        for c in range(w1_ref.shape[1] // FF_CHUNK):
            a = jnp.maximum(_dot(h, w1_ref[:, c * FF_CHUNK:(c + 1) * FF_CHUNK]), 0.0)
            f = f + _dot((a * a).astype(BF16), w2_ref[c * FF_CHUNK:(c + 1) * FF_CHUNK, :])
        x2 = x1 + gate2 * f
        if final:
            x2 = _rms(x2) * g_fin_ref[...]
        o_ref[0, rows, :] = x2


def _mix_out(layer, x, yg, ya, mod, g_oa, w_out, g_ffn, w1, w2, g_fin, final):
    b, s, d = x.shape
    tm = TM_OUT

    def par(a):
        return _layer_spec(a, layer, pipeline_mode=pl.Buffered(1))

    def whole(a):
        return pl.BlockSpec(a.shape, lambda i, j: (0,) * a.ndim, pipeline_mode=pl.Buffered(1))

    def tok(a):
        return pl.BlockSpec((1, tm, a.shape[-1]), lambda i, j: (i, j, 0))

    return pl.pallas_call(
        functools.partial(_mix_out_kernel, final=final, tm_sub=TM_SUB),
        grid=(b, s // tm),
        in_specs=[
            tok(x), tok(yg), tok(ya),
            _mod_spec(mod, layer),
            par(g_oa), whole(w_out), par(g_ffn), whole(w1), whole(w2), whole(g_fin),
        ],
        out_specs=tok(x),
        out_shape=jax.ShapeDtypeStruct(x.shape, F32),
        compiler_params=pltpu.CompilerParams(
            dimension_semantics=("arbitrary", "arbitrary"),
            vmem_limit_bytes=VMEM_LIMIT_BYTES),
        name="mix_out_final" if final else "mix_out",
    )(x, yg, ya, mod, g_oa, w_out, g_ffn, w1, w2, g_fin)


def _rotate_half_cols(w):
    half = w.shape[-1] // 2
    return jnp.concatenate([-w[..., half:], w[..., :half]], axis=-1)


def _prep_weights(w_in, mla_w_uq, mla_w_ukv, d_gmlp):
    depth, d, _ = w_in.shape
    q_rank = mla_w_uq.shape[1]
    kv_rank = mla_w_ukv.shape[1]
    o_kr = w_in.shape[-1] - QK_ROPE_DIM
    kr = w_in[..., o_kr:]
    w_uvt_in = jnp.swapaxes(w_in[..., :2 * d_gmlp], 1, 2).astype(BF16)
    w_in_ext = jnp.concatenate(
        [w_in[..., 2 * d_gmlp:o_kr], jnp.zeros((depth, d, QK_NOPE_DIM), F32), kr,
         _rotate_half_cols(kr)], axis=-1).astype(BF16)

    wq = mla_w_uq.reshape(depth, q_rank, MLA_HEADS, QK_NOPE_DIM + QK_ROPE_DIM)
    rope = wq[..., QK_NOPE_DIM:]
    w_uq_ext = jnp.concatenate([wq, _rotate_half_cols(rope)], axis=-1)
    w_uq_ext = w_uq_ext.reshape(depth, q_rank, MLA_HEADS * HEAD_SLAB).astype(BF16)

    wkv = mla_w_ukv.reshape(depth, kv_rank, MLA_HEADS, QK_NOPE_DIM + V_HEAD_DIM)
    w_k = jnp.concatenate(
        [wkv[..., :QK_NOPE_DIM],
         jnp.zeros((depth, kv_rank, MLA_HEADS, HEAD_SLAB - QK_NOPE_DIM), F32)], axis=-1)
    w_uk_ext = w_k.reshape(depth, kv_rank, MLA_HEADS * HEAD_SLAB).astype(BF16)
    w_v = jnp.concatenate(
        [wkv[..., QK_NOPE_DIM:], jnp.zeros((depth, kv_rank, MLA_HEADS, BF16_ROWS), F32)], axis=-1)
    w_uvt = jnp.swapaxes(w_v.reshape(depth, kv_rank, MLA_HEADS * V_SLAB), 1, 2).astype(BF16)
    v_one = (jnp.arange(MLA_HEADS * V_SLAB) % V_SLAB == V_HEAD_DIM).astype(F32).reshape(-1, 1)
    return w_uvt_in, w_in_ext, w_uq_ext, w_uk_ext, w_uvt, v_one


def kernel(x, c, positions, w_ada, b_ada, norm_mix_g, w_in, gmlp_ws, gmlp_bs, mla_q_norm_g,
           mla_kv_norm_g, mla_w_uq, mla_w_ukv, out_norm_gmlp_g, out_norm_mla_g, w_out,
           norm_ffn_g, w_ff1, w_ff2, final_norm_g):
    b, s, d = x.shape
    depth = w_ada.shape[0]
    d_gmlp = out_norm_gmlp_g.shape[-1]

    mod = _adaln(c, w_ada, b_ada).reshape(depth, b, N_MOD, d)
    cos_t, sin_t = _rope_tables(positions)
    w_uvt_in, w_in_ext, w_uq_ext, w_uk_ext, w_uvt, v_one = _prep_weights(
        w_in, mla_w_uq, mla_w_ukv, d_gmlp)
    wst = jnp.swapaxes(gmlp_ws, 2, 3)
    bias = gmlp_bs[:, :, None, :]
    g_og = out_norm_gmlp_g[:, :, None]
    g_fin = final_norm_g.reshape(1, d)

    def rows(g):
        return g[:, None, :]

    for l in range(depth):
        yg, q, k, vt, w_out_b, w1_b, w2_b = _mix_in(
            l, x, mod, cos_t, sin_t, rows(norm_mix_g), w_uvt_in, w_in_ext, wst, bias,
            rows(mla_q_norm_g), rows(mla_kv_norm_g), w_uq_ext, w_uk_ext, w_uvt, v_one, g_og,
            w_out, w_ff1, w_ff2)
        ya = _attention(q, k, vt)
        x = _mix_out(l, x, yg, ya, mod, rows(out_norm_mla_g), w_out_b, rows(norm_ffn_g),
                     w1_b, w2_b, g_fin, final=(l == depth - 1))
    return x
```

```python
import functools
import math

import jax
import jax.numpy as jnp
from jax import lax
from jax.experimental import pallas as pl
from jax.experimental.pallas import tpu as pltpu

F32 = jnp.float32
BF16 = jnp.bfloat16

EPS = 1e-6
ROPE_THETA = 10000.0
N_MOD = 6

GMLP_GROUPS = 8
CHUNK = 128
MLA_HEADS = 8
QK_NOPE_DIM = 64
QK_ROPE_DIM = 32
V_HEAD_DIM = 64
HEAD_SLAB = 128
LANES = 128
BF16_ROWS = 16
V_SLAB = V_HEAD_DIM + BF16_ROWS

VMEM_LIMIT_BYTES = 56 * 1024 * 1024

TM_IN = 1024
TM_SUB = 512
TM_OUT = 1024
TQ = 512
ATTN_HEADS_PER_STEP = 4
FF_CHUNK = 1024
ADA_TN = 1536


def _dot(a, b):
    return jnp.dot(a, b, preferred_element_type=F32)


def _rms(x):
    return x * lax.rsqrt(jnp.mean(x * x, axis=-1, keepdims=True) + EPS)


def _gelu_tanh(x):
    c = math.sqrt(2.0 / math.pi)
    half = 0.5 * x
    return half + half * jnp.tanh(x * (c + (c * 0.044715) * (x * x)))


def _prologue_kernel(c_ref, w_ref, b_ref, pos_ref, freq_ref, mod_ref, cos_ref, sin_ref):
    c = c_ref[...]
    c_act = (c / (1.0 + jnp.exp(-c))).astype(BF16)
    mod_ref[0] = _dot(c_act, w_ref[0].astype(BF16)) + b_ref[0]

    s_len = pos_ref.shape[2]
    pad = HEAD_SLAB - QK_NOPE_DIM - QK_ROPE_DIM
    for r in range(pos_ref.shape[0]):
        ang = freq_ref[...] * pos_ref[r].astype(F32)
        cs, sn = jnp.cos(ang), jnp.sin(ang)
        cos_ref[r] = jnp.concatenate(
            [jnp.ones((QK_NOPE_DIM, s_len), F32), cs, cs, jnp.zeros((pad, s_len), F32)], axis=0).T
        sin_ref[r] = jnp.concatenate(
            [jnp.zeros((QK_NOPE_DIM, s_len), F32), sn, sn, jnp.zeros((pad, s_len), F32)], axis=0).T


def _prologue(c, w_ada, b_ada, positions):
    depth, d, n = w_ada.shape
    b, s = positions.shape
    n_col = n // ADA_TN
    n_steps = depth * n_col
    rows = b // n_steps
    assert rows * n_steps == b
    freqs = ROPE_THETA ** (-jnp.arange(0, QK_ROPE_DIM, 2, dtype=F32) / QK_ROPE_DIM)
    table = jax.ShapeDtypeStruct((b, s, HEAD_SLAB), F32)
    return pl.pallas_call(
        _prologue_kernel,
        grid=(n_steps,),
        in_specs=[
            pl.BlockSpec((b, d), lambda t: (0, 0)),
            pl.BlockSpec((1, d, ADA_TN), lambda t: (t // n_col, 0, t % n_col)),
            pl.BlockSpec((1, 1, ADA_TN), lambda t: (t // n_col, 0, t % n_col)),
            pl.BlockSpec((rows, 1, s), lambda t: (t, 0, 0)),
            pl.BlockSpec((QK_ROPE_DIM // 2, 1), lambda t: (0, 0)),
        ],
        out_specs=[
            pl.BlockSpec((1, b, ADA_TN), lambda t: (t // n_col, 0, t % n_col)),
            pl.BlockSpec((rows, s, HEAD_SLAB), lambda t: (t, 0, 0)),
            pl.BlockSpec((rows, s, HEAD_SLAB), lambda t: (t, 0, 0)),
        ],
        out_shape=[jax.ShapeDtypeStruct((depth, b, n), F32), table, table],
        compiler_params=pltpu.CompilerParams(
            dimension_semantics=("arbitrary",),
            vmem_limit_bytes=VMEM_LIMIT_BYTES),
        name="prologue",
    )(c, w_ada, b_ada.reshape(depth, 1, n), positions.reshape(b, 1, s), freqs.reshape(-1, 1))


def _mix_in_kernel(x_ref, mod_ref, cos_ref, sin_ref, g_mix_ref, w_uvt_in_ref, w_in_ref, wst_ref,
                   bias_ref, g_q_ref, g_kv_ref, w_uq_ref, w_uk_ref, w_uvt_ref, v_one_ref,
                   g_og_ref, w_out_f_ref, w1_f_ref, w2_f_ref, yg_ref, q_ref, k_ref, vt_ref,
                   w_out_b_ref, w1_b_ref, w2_b_ref, *, d_gmlp, q_rank, kv_rank, q_scale, tm_sub):
    w_out_b_ref[...] = w_out_f_ref[...].astype(BF16)
    w1_b_ref[...] = w1_f_ref[...].astype(BF16)
    w2_b_ref[...] = w2_f_ref[...].astype(BF16)

    nt = (((1,), (1,)), ((), ()))
    mod = mod_ref[...]
    shift1, scale1 = mod[0:1], mod[1:2]
    gain1 = g_mix_ref[...] * (1.0 + scale1)
    n_chunk = tm_sub // CHUNK
    dg = d_gmlp // GMLP_GROUPS
    o_kv = q_rank
    o_kr = o_kv + kv_rank
    shift = HEAD_SLAB - QK_ROPE_DIM

    row = lax.broadcasted_iota(jnp.int32, (CHUNK, CHUNK), 0)
    col = lax.broadcasted_iota(jnp.int32, (CHUNK, CHUNK), 1)
    w_mix = [jnp.where(row <= col, wst_ref[g], 0.0).astype(BF16) for g in range(GMLP_GROUPS)]
    bias = [jnp.concatenate([bias_ref[g]] * n_chunk, axis=1) for g in range(GMLP_GROUPS)]

    n_sub = x_ref.shape[1] // tm_sub
    proj = []
    for sb in range(n_sub):
        rows = slice(sb * tm_sub, (sb + 1) * tm_sub)
        h = (_rms(x_ref[0, rows, :]) * gain1 + shift1).astype(BF16)
        proj.append((_dot(h, w_in_ref[...]),
                     lax.dot_general(w_uvt_in_ref[...], h, nt, preferred_element_type=F32)))

    def mla(sb):
        rows = slice(sb * tm_sub, (sb + 1) * tm_sub)
        z = proj[sb][0]
        cos_t = cos_ref[0, rows, :]
        sin_t = sin_ref[0, rows, :]
        c_q = (_rms(z[:, :o_kv]) * g_q_ref[...]).astype(BF16)
        qf = _dot(c_q, w_uq_ref[...])
        cos_q = cos_t * q_scale
        sin_q = sin_t * q_scale
        for hd in range(MLA_HEADS):
            qs = qf[:, hd * HEAD_SLAB:(hd + 1) * HEAD_SLAB]
            q_ref[0, hd, rows, :] = (qs * cos_q + pltpu.roll(qs, shift, 1) * sin_q).astype(BF16)

        c_kv = (_rms(z[:, o_kv:o_kr]) * g_kv_ref[...]).astype(BF16)
        kf = _dot(c_kv, w_uk_ref[...])
        kr = z[:, o_kr:]
        kr = kr * cos_t + pltpu.roll(kr, shift, 1) * sin_t
        for hd in range(MLA_HEADS):
            k_ref[0, hd, rows, :] = (kf[:, hd * HEAD_SLAB:(hd + 1) * HEAD_SLAB] + kr).astype(BF16)
        vt = lax.dot_general(w_uvt_ref[...], c_kv, nt,
                             preferred_element_type=F32)
        vt_ref[0, :, rows] = (vt + v_one_ref[...]).astype(BF16)

    def gmlp(sb):
        rows = slice(sb * tm_sub, (sb + 1) * tm_sub)
        uv_t = proj[sb][1]
        gu = _gelu_tanh(uv_t[:d_gmlp]).reshape(GMLP_GROUPS, dg, tm_sub)
        gv = _gelu_tanh(uv_t[d_gmlp:]).reshape(GMLP_GROUPS, dg, tm_sub)
        dv = gv - jnp.mean(gv, axis=1, keepdims=True)
        vn = (dv * lax.rsqrt(jnp.mean(dv * dv, axis=1, keepdims=True) + EPS)).astype(BF16)
        mixed = []
        for g in range(GMLP_GROUPS):
            lhs = jnp.concatenate([vn[g, :, c * CHUNK:(c + 1) * CHUNK] for c in range(n_chunk)], axis=0)
            out = _dot(lhs, w_mix[g])
            out = jnp.concatenate([out[c * dg:(c + 1) * dg] for c in range(n_chunk)], axis=1)
            mixed.append(out + bias[g])
        yg_t = (gu * jnp.stack(mixed)).reshape(d_gmlp, tm_sub)
        yg_t = yg_t * lax.rsqrt(jnp.mean(yg_t * yg_t, axis=0, keepdims=True) + EPS) * g_og_ref[...]
        yg_ref[0, rows, :] = yg_t.T.astype(BF16)

    for sb in range(n_sub):
        gmlp(sb)
    for sb in range(n_sub):
        mla(sb)


def _layer_spec(a, layer, **kw):
    return pl.BlockSpec((None,) + a.shape[1:], lambda i, j: (layer,) + (0,) * (a.ndim - 1), **kw)


def _mod_spec(mod, layer):
    return pl.BlockSpec((None, None) + mod.shape[2:], lambda i, j: (layer, i, 0, 0))


def _mix_in(layer, x, mod, cos_t, sin_t, g_mix, w_uvt_in, w_in, wst, bias, g_q, g_kv, w_uq, w_uk,
            w_uvt, v_one, g_og, w_out, w_ff1, w_ff2):
    b, s, d = x.shape
    n_j = s // TM_IN
    n_steps = b * n_j

    def slabbed(w):
        return w.reshape(w.shape[0], n_steps, w.shape[1] // n_steps, w.shape[2])

    def slab_in(w4):
        return pl.BlockSpec((None, None) + w4.shape[2:], lambda i, j: (layer, i * n_j + j, 0, 0))

    def slab_out(w4):
        return pl.BlockSpec((None,) + w4.shape[2:], lambda i, j: (i * n_j + j, 0, 0))

    f32_w = [slabbed(w) for w in (w_out, w_ff1, w_ff2)]
    d_gmlp = g_og.shape[1]
    q_rank = g_q.shape[-1]
    kv_rank = g_kv.shape[-1]
    tm = TM_IN
    q_scale = (QK_NOPE_DIM + QK_ROPE_DIM) ** -0.5 * math.log2(math.e)

    def par(a):
        return _layer_spec(a, layer)

    kern = functools.partial(_mix_in_kernel, d_gmlp=d_gmlp, q_rank=q_rank,
                             kv_rank=kv_rank, q_scale=q_scale, tm_sub=TM_SUB)
    slab = jax.ShapeDtypeStruct((b, MLA_HEADS, s, HEAD_SLAB), BF16)
    outs = pl.pallas_call(
        kern,
        grid=(b, n_j),
        in_specs=[
            pl.BlockSpec((1, tm, d), lambda i, j: (i, j, 0)),
            _mod_spec(mod, layer),
            pl.BlockSpec((1, tm, HEAD_SLAB), lambda i, j: (i, j, 0)),
            pl.BlockSpec((1, tm, HEAD_SLAB), lambda i, j: (i, j, 0)),
            par(g_mix), par(w_uvt_in), par(w_in), par(wst), par(bias),
            par(g_q), par(g_kv), par(w_uq), par(w_uk), par(w_uvt),
            pl.BlockSpec(v_one.shape, lambda i, j: (0, 0)), par(g_og),
        ] + [slab_in(w4) for w4 in f32_w],
        out_specs=[
            pl.BlockSpec((1, tm, d_gmlp), lambda i, j: (i, j, 0)),
            pl.BlockSpec((1, MLA_HEADS, tm, HEAD_SLAB), lambda i, j: (i, 0, j, 0)),
            pl.BlockSpec((1, MLA_HEADS, tm, HEAD_SLAB), lambda i, j: (i, 0, j, 0)),
            pl.BlockSpec((1, MLA_HEADS * V_SLAB, tm), lambda i, j: (i, 0, j)),
        ] + [slab_out(w4) for w4 in f32_w],
        out_shape=[
            jax.ShapeDtypeStruct((b, s, d_gmlp), BF16),
            slab, slab,
            jax.ShapeDtypeStruct((b, MLA_HEADS * V_SLAB, s), BF16),
        ] + [jax.ShapeDtypeStruct(w4.shape[1:], BF16) for w4 in f32_w],
        compiler_params=pltpu.CompilerParams(
            dimension_semantics=("arbitrary", "arbitrary"),
            vmem_limit_bytes=VMEM_LIMIT_BYTES),
        name="mix_in",
    )(x, mod, cos_t, sin_t, g_mix, w_uvt_in, w_in, wst, bias, g_q, g_kv, w_uq, w_uk, w_uvt, v_one,
      g_og, *f32_w)
    yg, q, k, vt = outs[:4]
    w_out_b, w1_b, w2_b = [o.reshape(w.shape[1:]) for o, w in zip(outs[4:], (w_out, w_ff1, w_ff2))]
    return yg, q, k, vt, w_out_b, w1_b, w2_b


def _attn_kernel(q_ref, k_ref, vt_ref, o_ref, *, tq):
    n_pair = q_ref.shape[1] // 2
    s_len = q_ref.shape[2]
    nt = (((1,), (1,)), ((), ()))
    half = tq // 2
    key = lax.broadcasted_iota(jnp.int32, (half, half), 0)
    qry = lax.broadcasted_iota(jnp.int32, (half, half), 1)
    causal = key <= qry
    n_blk = s_len // tq

    def qk(pair, c):
        q0, kmid, kend = c * tq, c * tq + half, (c + 1) * tq
        return [(lax.dot_general(k_ref[0, hd, :kmid, :], q_ref[0, hd, q0:kend, :], nt,
                                 preferred_element_type=F32),
                 lax.dot_general(k_ref[0, hd, kmid:kend, :], q_ref[0, hd, kmid:kend, :], nt,
                                 preferred_element_type=F32))
                for hd in (2 * pair, 2 * pair + 1)]

    jobs = [(pair, c) for pair in range(n_pair)
            for c in (range(n_blk) if pair % 2 == 0 else range(n_blk - 1, -1, -1))]
    nxt = qk(*jobs[0])
    for i, (pair, c) in enumerate(jobs):
        q0, kmid, kend = c * tq, c * tq + half, (c + 1) * tq
        scores = nxt
        if i + 1 < len(jobs):
            nxt = qk(*jobs[i + 1])
        halves = []
        for e in range(2):
            a, b = scores[e]
            top = jnp.concatenate([jnp.where(causal, a[q0:, :half], -1e30), a[q0:, half:]], axis=1)
            a = top if c == 0 else jnp.concatenate([a[:q0], top], axis=0)
            b = jnp.where(causal, b, -1e30)
            m = jnp.max(a, axis=0, keepdims=True)
            m = jnp.concatenate(
                [m[:, :half], jnp.maximum(m[:, half:], jnp.max(b, axis=0, keepdims=True))], axis=1)
            p_a = jnp.exp2(a - m).astype(BF16)
            p_b = jnp.exp2(b - m[:, half:]).astype(BF16)
            hd = 2 * pair + e
            vt = vt_ref.at[0, hd * V_SLAB:(hd + 1) * V_SLAB, :]
            o = _dot(vt[:, :kmid], p_a)
            o = jnp.concatenate([o[:, :half], o[:, half:] + _dot(vt[:, kmid:kend], p_b)], axis=1)
            halves.append(o[:V_HEAD_DIM] * (1.0 / o[V_HEAD_DIM:V_HEAD_DIM + 1]))
        o_ref[0, q0:kend, pair * LANES:(pair + 1) * LANES] = (
            jnp.concatenate(halves, axis=0).T.astype(o_ref.dtype))


def _attention(q, k, vt):
    b, hds, s, _ = q.shape
    hpb = ATTN_HEADS_PER_STEP
    return pl.pallas_call(
        functools.partial(_attn_kernel, tq=TQ),
        grid=(b, hds // hpb),
        in_specs=[
            pl.BlockSpec((1, hpb, s, HEAD_SLAB), lambda bi, j: (bi, j, 0, 0)),
            pl.BlockSpec((1, hpb, s, HEAD_SLAB), lambda bi, j: (bi, j, 0, 0)),
            pl.BlockSpec((1, hpb * V_SLAB, s), lambda bi, j: (bi, j, 0)),
        ],
        out_specs=pl.BlockSpec((1, s, hpb * V_HEAD_DIM), lambda bi, j: (bi, 0, j)),
        out_shape=jax.ShapeDtypeStruct((b, s, hds * V_HEAD_DIM), BF16),
        compiler_params=pltpu.CompilerParams(
            dimension_semantics=("arbitrary", "arbitrary"),
            vmem_limit_bytes=VMEM_LIMIT_BYTES),
        name="attn",
    )(q, k, vt)


def _mix_out_kernel(x_ref, yg_ref, ya_ref, mod_ref, g_oa_ref, w_out_ref, g_ffn_ref,
                    w1_ref, w2_ref, g_fin_ref, o_ref, *, final, tm_sub):
    mod = mod_ref[...]
    gate1, shift2, scale2, gate2 = mod[2:3], mod[3:4], mod[4:5], mod[5:6]
    gain2 = g_ffn_ref[...] * (1.0 + scale2)
    d_g = yg_ref.shape[-1]
    n_sub = x_ref.shape[1] // tm_sub

    mids = []
    for sb in range(n_sub):
        rows = slice(sb * tm_sub, (sb + 1) * tm_sub)
        ya = (_rms(ya_ref[0, rows, :].astype(F32)) * g_oa_ref[...]).astype(BF16)
        mixed = _dot(yg_ref[0, rows, :], w_out_ref[:d_g, :]) + _dot(ya, w_out_ref[d_g:, :])
        x1 = x_ref[0, rows, :] + gate1 * mixed
        mids.append((x1, (_rms(x1) * gain2 + shift2).astype(BF16)))

    for sb in range(n_sub):
        rows = slice(sb * tm_sub, (sb + 1) * tm_sub)
        x1, h = mids[sb]
        f = jnp.zeros_like(x1)
        for c in range(w1_ref.shape[1] // FF_CHUNK):
            a = jnp.maximum(_dot(h, w1_ref[:, c * FF_CHUNK:(c + 1) * FF_CHUNK]), 0.0)
            f = f + _dot((a * a).astype(BF16), w2_ref[c * FF_CHUNK:(c + 1) * FF_CHUNK, :])
        x2 = x1 + gate2 * f
        if final:
            x2 = _rms(x2) * g_fin_ref[...]
        o_ref[0, rows, :] = x2


def _mix_out(layer, x, yg, ya, mod, g_oa, w_out, g_ffn, w1, w2, g_fin, final):
    b, s, d = x.shape
    tm = TM_OUT

    def par(a):
        return _layer_spec(a, layer, pipeline_mode=pl.Buffered(1))

    def whole(a):
        return pl.BlockSpec(a.shape, lambda i, j: (0,) * a.ndim, pipeline_mode=pl.Buffered(1))

    def tok(a):
        return pl.BlockSpec((1, tm, a.shape[-1]), lambda i, j: (i, j, 0))

    return pl.pallas_call(
        functools.partial(_mix_out_kernel, final=final, tm_sub=TM_SUB),
        grid=(b, s // tm),
        in_specs=[
            tok(x), tok(yg), tok(ya),
            _mod_spec(mod, layer),
            par(g_oa), whole(w_out), par(g_ffn), whole(w1), whole(w2), whole(g_fin),
        ],
        out_specs=tok(x),
        out_shape=jax.ShapeDtypeStruct(x.shape, F32),
        compiler_params=pltpu.CompilerParams(
            dimension_semantics=("arbitrary", "arbitrary"),
            vmem_limit_bytes=VMEM_LIMIT_BYTES),
        name="mix_out_final" if final else "mix_out",
    )(x, yg, ya, mod, g_oa, w_out, g_ffn, w1, w2, g_fin)


def _rotate_half_cols(w):
    half = w.shape[-1] // 2
    return jnp.concatenate([-w[..., half:], w[..., :half]], axis=-1)


def _prep_weights(w_in, mla_w_uq, mla_w_ukv, d_gmlp):
    depth, d, _ = w_in.shape
    q_rank = mla_w_uq.shape[1]
    kv_rank = mla_w_ukv.shape[1]
    o_kr = w_in.shape[-1] - QK_ROPE_DIM
    kr = w_in[..., o_kr:]
    w_uvt_in = jnp.swapaxes(w_in[..., :2 * d_gmlp], 1, 2).astype(BF16)
    w_in_ext = jnp.concatenate(
        [w_in[..., 2 * d_gmlp:o_kr], jnp.zeros((depth, d, QK_NOPE_DIM), F32), kr,
         _rotate_half_cols(kr)], axis=-1).astype(BF16)

    wq = mla_w_uq.reshape(depth, q_rank, MLA_HEADS, QK_NOPE_DIM + QK_ROPE_DIM)
    rope = wq[..., QK_NOPE_DIM:]
    w_uq_ext = jnp.concatenate([wq, _rotate_half_cols(rope)], axis=-1)
    w_uq_ext = w_uq_ext.reshape(depth, q_rank, MLA_HEADS * HEAD_SLAB).astype(BF16)

    wkv = mla_w_ukv.reshape(depth, kv_rank, MLA_HEADS, QK_NOPE_DIM + V_HEAD_DIM)
    w_k = jnp.concatenate(
        [wkv[..., :QK_NOPE_DIM],
         jnp.zeros((depth, kv_rank, MLA_HEADS, HEAD_SLAB - QK_NOPE_DIM), F32)], axis=-1)
    w_uk_ext = w_k.reshape(depth, kv_rank, MLA_HEADS * HEAD_SLAB).astype(BF16)
    w_v = jnp.concatenate(
        [wkv[..., QK_NOPE_DIM:], jnp.zeros((depth, kv_rank, MLA_HEADS, BF16_ROWS), F32)], axis=-1)
    w_uvt = jnp.swapaxes(w_v.reshape(depth, kv_rank, MLA_HEADS * V_SLAB), 1, 2).astype(BF16)
    v_one = (jnp.arange(MLA_HEADS * V_SLAB) % V_SLAB == V_HEAD_DIM).astype(F32).reshape(-1, 1)
    return w_uvt_in, w_in_ext, w_uq_ext, w_uk_ext, w_uvt, v_one


def kernel(x, c, positions, w_ada, b_ada, norm_mix_g, w_in, gmlp_ws, gmlp_bs, mla_q_norm_g,
           mla_kv_norm_g, mla_w_uq, mla_w_ukv, out_norm_gmlp_g, out_norm_mla_g, w_out,
           norm_ffn_g, w_ff1, w_ff2, final_norm_g):
    b, s, d = x.shape
    depth = w_ada.shape[0]
    d_gmlp = out_norm_gmlp_g.shape[-1]

    mod, cos_t, sin_t = _prologue(c, w_ada, b_ada, positions)
    mod = mod.reshape(depth, b, N_MOD, d)
    w_uvt_in, w_in_ext, w_uq_ext, w_uk_ext, w_uvt, v_one = _prep_weights(
        w_in, mla_w_uq, mla_w_ukv, d_gmlp)
    wst = jnp.swapaxes(gmlp_ws, 2, 3)
    bias = gmlp_bs[:, :, None, :]
    g_og = out_norm_gmlp_g[:, :, None]
    g_fin = final_norm_g.reshape(1, d)

    def rows(g):
        return g[:, None, :]

    for l in range(depth):
        yg, q, k, vt, w_out_b, w1_b, w2_b = _mix_in(
            l, x, mod, cos_t, sin_t, rows(norm_mix_g), w_uvt_in, w_in_ext, wst, bias,
            rows(mla_q_norm_g), rows(mla_kv_norm_g), w_uq_ext, w_uk_ext, w_uvt, v_one, g_og,
            w_out, w_ff1, w_ff2)
        ya = _attention(q, k, vt)
        x = _mix_out(l, x, yg, ya, mod, rows(out_norm_mla_g), w_out_b, rows(norm_ffn_g),
                     w1_b, w2_b, g_fin, final=(l == depth - 1))
    return x
```

```python
import functools
import math

import jax
import jax.numpy as jnp
from jax import lax
from jax.experimental import pallas as pl
from jax.experimental.pallas import tpu as pltpu

F32 = jnp.float32
BF16 = jnp.bfloat16

EPS = 1e-6
ROPE_THETA = 10000.0
N_MOD = 6

GMLP_GROUPS = 8
CHUNK = 128
MLA_HEADS = 8
QK_NOPE_DIM = 64
QK_ROPE_DIM = 32
V_HEAD_DIM = 64
HEAD_SLAB = 128
LANES = 128
BF16_ROWS = 16
V_SLAB = V_HEAD_DIM + BF16_ROWS

VMEM_LIMIT_BYTES = 56 * 1024 * 1024

TM_IN = 1024
TM_SUB = 512
TM_OUT = 1024
TQ = 512
ATTN_HEADS_PER_STEP = 4
ADA_TN = 1536


def _dot(a, b):
    return jnp.dot(a, b, preferred_element_type=F32)


def _rms(x):
    return x * lax.rsqrt(jnp.mean(x * x, axis=-1, keepdims=True) + EPS)


def _gelu_tanh(x):
    c = math.sqrt(2.0 / math.pi)
    half = 0.5 * x
    return half + half * jnp.tanh(x * (c + (c * 0.044715) * (x * x)))


def _prologue_kernel(c_ref, w_ref, b_ref, pos_ref, freq_ref, mod_ref, cos_ref, sin_ref):
    c = c_ref[...]
    c_act = (c / (1.0 + jnp.exp(-c))).astype(BF16)
    mod_ref[0] = _dot(c_act, w_ref[0].astype(BF16)) + b_ref[0]

    s_len = pos_ref.shape[2]
    pad = HEAD_SLAB - QK_NOPE_DIM - QK_ROPE_DIM
    for r in range(pos_ref.shape[0]):
        ang = freq_ref[...] * pos_ref[r].astype(F32)
        cs, sn = jnp.cos(ang), jnp.sin(ang)
        cos_ref[r] = jnp.concatenate(
            [jnp.ones((QK_NOPE_DIM, s_len), F32), cs, cs, jnp.zeros((pad, s_len), F32)], axis=0).T
        sin_ref[r] = jnp.concatenate(
            [jnp.zeros((QK_NOPE_DIM, s_len), F32), sn, sn, jnp.zeros((pad, s_len), F32)], axis=0).T


def _prologue(c, w_ada, b_ada, positions):
    depth, d, n = w_ada.shape
    b, s = positions.shape
    n_col = n // ADA_TN
    n_steps = depth * n_col
    rows = b // n_steps
    assert rows * n_steps == b
    freqs = ROPE_THETA ** (-jnp.arange(0, QK_ROPE_DIM, 2, dtype=F32) / QK_ROPE_DIM)
    table = jax.ShapeDtypeStruct((b, s, HEAD_SLAB), F32)
    return pl.pallas_call(
        _prologue_kernel,
        grid=(n_steps,),
        in_specs=[
            pl.BlockSpec((b, d), lambda t: (0, 0)),
            pl.BlockSpec((1, d, ADA_TN), lambda t: (t // n_col, 0, t % n_col)),
            pl.BlockSpec((1, 1, ADA_TN), lambda t: (t // n_col, 0, t % n_col)),
            pl.BlockSpec((rows, 1, s), lambda t: (t, 0, 0)),
            pl.BlockSpec((QK_ROPE_DIM // 2, 1), lambda t: (0, 0)),
        ],
        out_specs=[
            pl.BlockSpec((1, b, ADA_TN), lambda t: (t // n_col, 0, t % n_col)),
            pl.BlockSpec((rows, s, HEAD_SLAB), lambda t: (t, 0, 0)),
            pl.BlockSpec((rows, s, HEAD_SLAB), lambda t: (t, 0, 0)),
        ],
        out_shape=[jax.ShapeDtypeStruct((depth, b, n), F32), table, table],
        compiler_params=pltpu.CompilerParams(
            dimension_semantics=("arbitrary",),
            vmem_limit_bytes=VMEM_LIMIT_BYTES),
        name="prologue",
    )(c, w_ada, b_ada.reshape(depth, 1, n), positions.reshape(b, 1, s), freqs.reshape(-1, 1))


def _mix_in_kernel(x_ref, mod_ref, cos_ref, sin_ref, g_mix_ref, w_uvt_in_ref, w_in_ref, wst_ref,
                   bias_ref, g_q_ref, g_kv_ref, w_uq_ref, w_uk_ref, w_uvt_ref, v_one_ref,
                   g_og_ref, w_out_f_ref, w1_f_ref, w2_f_ref, yg_ref, q_ref, k_ref, vt_ref,
                   w_out_b_ref, w1_b_ref, w2_b_ref, *, d_gmlp, q_rank, kv_rank, q_scale, tm_sub):
    w_out_b_ref[...] = w_out_f_ref[...].astype(BF16)
    w1_b_ref[...] = w1_f_ref[...].astype(BF16)
    w2_b_ref[...] = w2_f_ref[...].astype(BF16)

    nt = (((1,), (1,)), ((), ()))
    mod = mod_ref[...]
    shift1, scale1 = mod[0:1], mod[1:2]
    gain1 = g_mix_ref[...] * (1.0 + scale1)
    n_chunk = tm_sub // CHUNK
    dg = d_gmlp // GMLP_GROUPS
    o_kv = q_rank
    o_kr = o_kv + kv_rank
    shift = HEAD_SLAB - QK_ROPE_DIM

    row = lax.broadcasted_iota(jnp.int32, (CHUNK, CHUNK), 0)
    col = lax.broadcasted_iota(jnp.int32, (CHUNK, CHUNK), 1)
    w_mix = [jnp.where(row <= col, wst_ref[g], 0.0).astype(BF16) for g in range(GMLP_GROUPS)]
    bias = [jnp.concatenate([bias_ref[g]] * n_chunk, axis=1) for g in range(GMLP_GROUPS)]

    n_sub = x_ref.shape[1] // tm_sub
    proj = []
    for sb in range(n_sub):
        rows = slice(sb * tm_sub, (sb + 1) * tm_sub)
        h = (_rms(x_ref[0, rows, :]) * gain1 + shift1).astype(BF16)
        proj.append((_dot(h, w_in_ref[...]),
                     lax.dot_general(w_uvt_in_ref[...], h, nt, preferred_element_type=F32)))

    def mla(sb):
        rows = slice(sb * tm_sub, (sb + 1) * tm_sub)
        z = proj[sb][0]
        cos_t = cos_ref[0, rows, :]
        sin_t = sin_ref[0, rows, :]
        c_q = (_rms(z[:, :o_kv]) * g_q_ref[...]).astype(BF16)
        qf = _dot(c_q, w_uq_ref[...])
        cos_q = cos_t * q_scale
        sin_q = sin_t * q_scale
        for hd in range(MLA_HEADS):
            qs = qf[:, hd * HEAD_SLAB:(hd + 1) * HEAD_SLAB]
            q_ref[0, hd, rows, :] = (qs * cos_q + pltpu.roll(qs, shift, 1) * sin_q).astype(BF16)

        c_kv = (_rms(z[:, o_kv:o_kr]) * g_kv_ref[...]).astype(BF16)
        kf = _dot(c_kv, w_uk_ref[...])
        kr = z[:, o_kr:]
        kr = kr * cos_t + pltpu.roll(kr, shift, 1) * sin_t
        for hd in range(MLA_HEADS):
            k_ref[0, hd, rows, :] = (kf[:, hd * HEAD_SLAB:(hd + 1) * HEAD_SLAB] + kr).astype(BF16)
        vt = lax.dot_general(w_uvt_ref[...], c_kv, nt,
                             preferred_element_type=F32)
        vt_ref[0, :, rows] = (vt + v_one_ref[...]).astype(BF16)

    def gmlp(sb):
        rows = slice(sb * tm_sub, (sb + 1) * tm_sub)
        uv_t = proj[sb][1]
        gu = _gelu_tanh(uv_t[:d_gmlp]).reshape(GMLP_GROUPS, dg, tm_sub)
        gv = _gelu_tanh(uv_t[d_gmlp:]).reshape(GMLP_GROUPS, dg, tm_sub)
        dv = gv - jnp.mean(gv, axis=1, keepdims=True)
        vn = (dv * lax.rsqrt(jnp.mean(dv * dv, axis=1, keepdims=True) + EPS)).astype(BF16)
        mixed = []
        for g in range(GMLP_GROUPS):
            lhs = jnp.concatenate([vn[g, :, c * CHUNK:(c + 1) * CHUNK] for c in range(n_chunk)], axis=0)
            out = _dot(lhs, w_mix[g])
            out = jnp.concatenate([out[c * dg:(c + 1) * dg] for c in range(n_chunk)], axis=1)
            mixed.append(out + bias[g])
        yg_t = (gu * jnp.stack(mixed)).reshape(d_gmlp, tm_sub)
        yg_t = yg_t * lax.rsqrt(jnp.mean(yg_t * yg_t, axis=0, keepdims=True) + EPS) * g_og_ref[...]
        yg_ref[0, rows, :] = yg_t.T.astype(BF16)

    for sb in range(n_sub):
        gmlp(sb)
    for sb in range(n_sub):
        mla(sb)


def _layer_spec(a, layer, **kw):
    return pl.BlockSpec((None,) + a.shape[1:], lambda i, j: (layer,) + (0,) * (a.ndim - 1), **kw)


def _mod_spec(mod, layer):
    return pl.BlockSpec((None, None) + mod.shape[2:], lambda i, j: (layer, i, 0, 0))


def _mix_in(layer, x, mod, cos_t, sin_t, g_mix, w_uvt_in, w_in, wst, bias, g_q, g_kv, w_uq, w_uk,
            w_uvt, v_one, g_og, w_out, w_ff1, w_ff2):
    b, s, d = x.shape
    n_j = s // TM_IN
    n_steps = b * n_j

    def slabbed(w):
        return w.reshape(w.shape[0], n_steps, w.shape[1] // n_steps, w.shape[2])

    def slab_in(w4):
        return pl.BlockSpec((None, None) + w4.shape[2:], lambda i, j: (layer, i * n_j + j, 0, 0))

    def slab_out(w4):
        return pl.BlockSpec((None,) + w4.shape[2:], lambda i, j: (i * n_j + j, 0, 0))

    f32_w = [slabbed(w) for w in (w_out, w_ff1, w_ff2)]
    d_gmlp = g_og.shape[1]
    q_rank = g_q.shape[-1]
    kv_rank = g_kv.shape[-1]
    tm = TM_IN
    q_scale = (QK_NOPE_DIM + QK_ROPE_DIM) ** -0.5 * math.log2(math.e)

    def par(a):
        return _layer_spec(a, layer)

    kern = functools.partial(_mix_in_kernel, d_gmlp=d_gmlp, q_rank=q_rank,
                             kv_rank=kv_rank, q_scale=q_scale, tm_sub=TM_SUB)
    slab = jax.ShapeDtypeStruct((b, MLA_HEADS, s, HEAD_SLAB), BF16)
    outs = pl.pallas_call(
        kern,
        grid=(b, n_j),
        in_specs=[
            pl.BlockSpec((1, tm, d), lambda i, j: (i, j, 0)),
            _mod_spec(mod, layer),
            pl.BlockSpec((1, tm, HEAD_SLAB), lambda i, j: (i, j, 0)),
            pl.BlockSpec((1, tm, HEAD_SLAB), lambda i, j: (i, j, 0)),
            par(g_mix), par(w_uvt_in), par(w_in), par(wst), par(bias),
            par(g_q), par(g_kv), par(w_uq), par(w_uk), par(w_uvt),
            pl.BlockSpec(v_one.shape, lambda i, j: (0, 0)), par(g_og),
        ] + [slab_in(w4) for w4 in f32_w],
        out_specs=[
            pl.BlockSpec((1, tm, d_gmlp), lambda i, j: (i, j, 0)),
            pl.BlockSpec((1, MLA_HEADS, tm, HEAD_SLAB), lambda i, j: (i, 0, j, 0)),
            pl.BlockSpec((1, MLA_HEADS, tm, HEAD_SLAB), lambda i, j: (i, 0, j, 0)),
            pl.BlockSpec((1, MLA_HEADS * V_SLAB, tm), lambda i, j: (i, 0, j)),
        ] + [slab_out(w4) for w4 in f32_w],
        out_shape=[
            jax.ShapeDtypeStruct((b, s, d_gmlp), BF16),
            slab, slab,
            jax.ShapeDtypeStruct((b, MLA_HEADS * V_SLAB, s), BF16),
        ] + [jax.ShapeDtypeStruct(w4.shape[1:], BF16) for w4 in f32_w],
        compiler_params=pltpu.CompilerParams(
            dimension_semantics=("arbitrary", "arbitrary"),
            vmem_limit_bytes=VMEM_LIMIT_BYTES),
        name="mix_in",
    )(x, mod, cos_t, sin_t, g_mix, w_uvt_in, w_in, wst, bias, g_q, g_kv, w_uq, w_uk, w_uvt, v_one,
      g_og, *f32_w)
    yg, q, k, vt = outs[:4]
    w_out_b, w1_b, w2_b = [o.reshape(w.shape[1:]) for o, w in zip(outs[4:], (w_out, w_ff1, w_ff2))]
    return yg, q, k, vt, w_out_b, w1_b, w2_b


def _attn_kernel(q_ref, k_ref, vt_ref, o_ref, *, tq):
    n_pair = q_ref.shape[1] // 2
    s_len = q_ref.shape[2]
    nt = (((1,), (1,)), ((), ()))
    half = tq // 2
    key = lax.broadcasted_iota(jnp.int32, (half, half), 0)
    qry = lax.broadcasted_iota(jnp.int32, (half, half), 1)
    causal = key <= qry
    n_blk = s_len // tq

    def qk(pair, c):
        q0, kmid, kend = c * tq, c * tq + half, (c + 1) * tq
        return [(lax.dot_general(k_ref[0, hd, :kmid, :], q_ref[0, hd, q0:kend, :], nt,
                                 preferred_element_type=F32),
                 lax.dot_general(k_ref[0, hd, kmid:kend, :], q_ref[0, hd, kmid:kend, :], nt,
                                 preferred_element_type=F32))
                for hd in (2 * pair, 2 * pair + 1)]

    jobs = [(pair, c) for pair in range(n_pair)
            for c in (range(n_blk) if pair % 2 == 0 else range(n_blk - 1, -1, -1))]
    nxt = qk(*jobs[0])
    for i, (pair, c) in enumerate(jobs):
        q0, kmid, kend = c * tq, c * tq + half, (c + 1) * tq
        scores = nxt
        if i + 1 < len(jobs):
            nxt = qk(*jobs[i + 1])
        halves = []
        for e in range(2):
            a, b = scores[e]
            top = jnp.concatenate([jnp.where(causal, a[q0:, :half], -1e30), a[q0:, half:]], axis=1)
            a = top if c == 0 else jnp.concatenate([a[:q0], top], axis=0)
            b = jnp.where(causal, b, -1e30)
            m = jnp.max(a, axis=0, keepdims=True)
            m = jnp.concatenate(
                [m[:, :half], jnp.maximum(m[:, half:], jnp.max(b, axis=0, keepdims=True))], axis=1)
            p_a = jnp.exp2(a - m).astype(BF16)
            p_b = jnp.exp2(b - m[:, half:]).astype(BF16)
            hd = 2 * pair + e
            vt = vt_ref.at[0, hd * V_SLAB:(hd + 1) * V_SLAB, :]
            o = _dot(vt[:, :kmid], p_a)
            o = jnp.concatenate([o[:, :half], o[:, half:] + _dot(vt[:, kmid:kend], p_b)], axis=1)
            halves.append(o[:V_HEAD_DIM] * (1.0 / o[V_HEAD_DIM:V_HEAD_DIM + 1]))
        o_ref[0, q0:kend, pair * LANES:(pair + 1) * LANES] = (
            jnp.concatenate(halves, axis=0).T.astype(o_ref.dtype))


def _attention(q, k, vt):
    b, hds, s, _ = q.shape
    hpb = ATTN_HEADS_PER_STEP
    return pl.pallas_call(
        functools.partial(_attn_kernel, tq=TQ),
        grid=(b, hds // hpb),
        in_specs=[
            pl.BlockSpec((1, hpb, s, HEAD_SLAB), lambda bi, j: (bi, j, 0, 0)),
            pl.BlockSpec((1, hpb, s, HEAD_SLAB), lambda bi, j: (bi, j, 0, 0)),
            pl.BlockSpec((1, hpb * V_SLAB, s), lambda bi, j: (bi, j, 0)),
        ],
        out_specs=pl.BlockSpec((1, s, hpb * V_HEAD_DIM), lambda bi, j: (bi, 0, j)),
        out_shape=jax.ShapeDtypeStruct((b, s, hds * V_HEAD_DIM), BF16),
        compiler_params=pltpu.CompilerParams(
            dimension_semantics=("arbitrary", "arbitrary"),
            vmem_limit_bytes=VMEM_LIMIT_BYTES),
        name="attn",
    )(q, k, vt)


def _mix_out_kernel(x_ref, yg_ref, ya_ref, mod_ref, g_oa_ref, w_out_ref, g_ffn_ref,
                    w1_ref, w2_ref, g_fin_ref, o_ref, *, final, tm_sub):
    mod = mod_ref[...]
    gate1, shift2, scale2, gate2 = mod[2:3], mod[3:4], mod[4:5], mod[5:6]
    gain2 = g_ffn_ref[...] * (1.0 + scale2)
    d_g = yg_ref.shape[-1]
    n_sub = x_ref.shape[1] // tm_sub

    mids = []
    for sb in range(n_sub):
        rows = slice(sb * tm_sub, (sb + 1) * tm_sub)
        ya = (_rms(ya_ref[0, rows, :].astype(F32)) * g_oa_ref[...]).astype(BF16)
        mixed = _dot(yg_ref[0, rows, :], w_out_ref[:d_g, :]) + _dot(ya, w_out_ref[d_g:, :])
        x1 = x_ref[0, rows, :] + gate1 * mixed
        mids.append((x1, (_rms(x1) * gain2 + shift2).astype(BF16)))

    for sb in range(n_sub):
        rows = slice(sb * tm_sub, (sb + 1) * tm_sub)
        x1, h = mids[sb]
        a = jnp.maximum(_dot(h, w1_ref[...]), 0.0)
        x2 = x1 + gate2 * _dot((a * a).astype(BF16), w2_ref[...])
        if final:
            x2 = _rms(x2) * g_fin_ref[...]
        o_ref[0, rows, :] = x2


def _mix_out(layer, x, yg, ya, mod, g_oa, w_out, g_ffn, w1, w2, g_fin, final):
    b, s, d = x.shape
    tm = TM_OUT

    def par(a):
        return _layer_spec(a, layer, pipeline_mode=pl.Buffered(1))

    def whole(a):
        return pl.BlockSpec(a.shape, lambda i, j: (0,) * a.ndim, pipeline_mode=pl.Buffered(1))

    def tok(a):
        return pl.BlockSpec((1, tm, a.shape[-1]), lambda i, j: (i, j, 0))

    return pl.pallas_call(
        functools.partial(_mix_out_kernel, final=final, tm_sub=TM_SUB),
        grid=(b, s // tm),
        in_specs=[
            tok(x), tok(yg), tok(ya),
            _mod_spec(mod, layer),
            par(g_oa), whole(w_out), par(g_ffn), whole(w1), whole(w2), whole(g_fin),
        ],
        out_specs=tok(x),
        out_shape=jax.ShapeDtypeStruct(x.shape, F32),
        compiler_params=pltpu.CompilerParams(
            dimension_semantics=("arbitrary", "arbitrary"),
            vmem_limit_bytes=VMEM_LIMIT_BYTES),
        name="mix_out_final" if final else "mix_out",
    )(x, yg, ya, mod, g_oa, w_out, g_ffn, w1, w2, g_fin)


def _rotate_half_cols(w):
    half = w.shape[-1] // 2
    return jnp.concatenate([-w[..., half:], w[..., :half]], axis=-1)


def _prep_weights(w_in, mla_w_uq, mla_w_ukv, d_gmlp):
    depth, d, _ = w_in.shape
    q_rank = mla_w_uq.shape[1]
    kv_rank = mla_w_ukv.shape[1]
    o_kr = w_in.shape[-1] - QK_ROPE_DIM
    kr = w_in[..., o_kr:]
    w_uvt_in = jnp.swapaxes(w_in[..., :2 * d_gmlp], 1, 2).astype(BF16)
    w_in_ext = jnp.concatenate(
        [w_in[..., 2 * d_gmlp:o_kr], jnp.zeros((depth, d, QK_NOPE_DIM), F32), kr,
         _rotate_half_cols(kr)], axis=-1).astype(BF16)

    wq = mla_w_uq.reshape(depth, q_rank, MLA_HEADS, QK_NOPE_DIM + QK_ROPE_DIM)
    rope = wq[..., QK_NOPE_DIM:]
    w_uq_ext = jnp.concatenate([wq, _rotate_half_cols(rope)], axis=-1)
    w_uq_ext = w_uq_ext.reshape(depth, q_rank, MLA_HEADS * HEAD_SLAB).astype(BF16)

    wkv = mla_w_ukv.reshape(depth, kv_rank, MLA_HEADS, QK_NOPE_DIM + V_HEAD_DIM)
    w_k = jnp.concatenate(
        [wkv[..., :QK_NOPE_DIM],
         jnp.zeros((depth, kv_rank, MLA_HEADS, HEAD_SLAB - QK_NOPE_DIM), F32)], axis=-1)
    w_uk_ext = w_k.reshape(depth, kv_rank, MLA_HEADS * HEAD_SLAB).astype(BF16)
    w_v = jnp.concatenate(
        [wkv[..., QK_NOPE_DIM:], jnp.zeros((depth, kv_rank, MLA_HEADS, BF16_ROWS), F32)], axis=-1)
    w_uvt = jnp.swapaxes(w_v.reshape(depth, kv_rank, MLA_HEADS * V_SLAB), 1, 2).astype(BF16)
    v_one = (jnp.arange(MLA_HEADS * V_SLAB) % V_SLAB == V_HEAD_DIM).astype(F32).reshape(-1, 1)
    return w_uvt_in, w_in_ext, w_uq_ext, w_uk_ext, w_uvt, v_one


def kernel(x, c, positions, w_ada, b_ada, norm_mix_g, w_in, gmlp_ws, gmlp_bs, mla_q_norm_g,
           mla_kv_norm_g, mla_w_uq, mla_w_ukv, out_norm_gmlp_g, out_norm_mla_g, w_out,
           norm_ffn_g, w_ff1, w_ff2, final_norm_g):
    b, s, d = x.shape
    depth = w_ada.shape[0]
    d_gmlp = out_norm_gmlp_g.shape[-1]

    mod, cos_t, sin_t = _prologue(c, w_ada, b_ada, positions)
    mod = mod.reshape(depth, b, N_MOD, d)
    w_uvt_in, w_in_ext, w_uq_ext, w_uk_ext, w_uvt, v_one = _prep_weights(
        w_in, mla_w_uq, mla_w_ukv, d_gmlp)
    wst = jnp.swapaxes(gmlp_ws, 2, 3)
    bias = gmlp_bs[:, :, None, :]
    g_og = out_norm_gmlp_g[:, :, None]
    g_fin = final_norm_g.reshape(1, d)

    def rows(g):
        return g[:, None, :]

    for l in range(depth):
        yg, q, k, vt, w_out_b, w1_b, w2_b = _mix_in(
            l, x, mod, cos_t, sin_t, rows(norm_mix_g), w_uvt_in, w_in_ext, wst, bias,
            rows(mla_q_norm_g), rows(mla_kv_norm_g), w_uq_ext, w_uk_ext, w_uvt, v_one, g_og,
            w_out, w_ff1, w_ff2)
        ya = _attention(q, k, vt)
        x = _mix_out(l, x, yg, ya, mod, rows(out_norm_mla_g), w_out_b, rows(norm_ffn_g),
                     w1_b, w2_b, g_fin, final=(l == depth - 1))
    return x
```

```python
import functools
import math

import jax
import jax.numpy as jnp
from jax import lax
from jax.experimental import pallas as pl
from jax.experimental.pallas import tpu as pltpu

F32 = jnp.float32
BF16 = jnp.bfloat16

EPS = 1e-6
ROPE_THETA = 10000.0
N_MOD = 6

GMLP_GROUPS = 8
CHUNK = 128
MLA_HEADS = 8
QK_NOPE_DIM = 64
QK_ROPE_DIM = 32
V_HEAD_DIM = 64
HEAD_SLAB = 128
LANES = 128
BF16_ROWS = 16
V_SLAB = V_HEAD_DIM + BF16_ROWS

VMEM_LIMIT_BYTES = 56 * 1024 * 1024

TM_IN = 1024
TM_SUB = 512
TM_OUT = 1024
TQ = 512
ATTN_HEADS_PER_STEP = 4
ADA_TN = 1536


def _dot(a, b):
    return jnp.dot(a, b, preferred_element_type=F32)


def _rms(x):
    return x * lax.rsqrt(jnp.mean(x * x, axis=-1, keepdims=True) + EPS)


def _gelu_tanh(x):
    c = math.sqrt(2.0 / math.pi)
    half = 0.5 * x
    return half + half * jnp.tanh(x * (c + (c * 0.044715) * (x * x)))


def _prologue_kernel(c_ref, w_ref, b_ref, pos_ref, freq_ref, w_uv_ref, mod_ref, cos_ref, sin_ref,
                     w_uvt_ref):
    w_uvt_ref[...] = w_uv_ref[...].T.astype(BF16)

    c = c_ref[...]
    c_act = (c / (1.0 + jnp.exp(-c))).astype(BF16)
    mod_ref[0] = _dot(c_act, w_ref[0].astype(BF16)) + b_ref[0]

    s_len = pos_ref.shape[2]
    pad = HEAD_SLAB - QK_NOPE_DIM - QK_ROPE_DIM
    for r in range(pos_ref.shape[0]):
        ang = freq_ref[...] * pos_ref[r].astype(F32)
        cs, sn = jnp.cos(ang), jnp.sin(ang)
        cos_ref[r] = jnp.concatenate(
            [jnp.ones((QK_NOPE_DIM, s_len), F32), cs, cs, jnp.zeros((pad, s_len), F32)], axis=0).T
        sin_ref[r] = jnp.concatenate(
            [jnp.zeros((QK_NOPE_DIM, s_len), F32), sn, sn, jnp.zeros((pad, s_len), F32)], axis=0).T


def _prologue(c, w_ada, b_ada, positions, w_in, d_uv):
    depth, d, n = w_ada.shape
    b, s = positions.shape
    n_col = n // ADA_TN
    n_steps = depth * n_col
    rows = b // n_steps
    uv_cols = d_uv // n_col
    assert rows * n_steps == b and uv_cols * n_col == d_uv and uv_cols % LANES == 0
    freqs = ROPE_THETA ** (-jnp.arange(0, QK_ROPE_DIM, 2, dtype=F32) / QK_ROPE_DIM)
    table = jax.ShapeDtypeStruct((b, s, HEAD_SLAB), F32)
    return pl.pallas_call(
        _prologue_kernel,
        grid=(n_steps,),
        in_specs=[
            pl.BlockSpec((b, d), lambda t: (0, 0)),
            pl.BlockSpec((1, d, ADA_TN), lambda t: (t // n_col, 0, t % n_col)),
            pl.BlockSpec((1, 1, ADA_TN), lambda t: (t // n_col, 0, t % n_col)),
            pl.BlockSpec((rows, 1, s), lambda t: (t, 0, 0)),
            pl.BlockSpec((QK_ROPE_DIM // 2, 1), lambda t: (0, 0)),
            pl.BlockSpec((None, d, uv_cols), lambda t: (t // n_col, 0, t % n_col)),
        ],
        out_specs=[
            pl.BlockSpec((1, b, ADA_TN), lambda t: (t // n_col, 0, t % n_col)),
            pl.BlockSpec((rows, s, HEAD_SLAB), lambda t: (t, 0, 0)),
            pl.BlockSpec((rows, s, HEAD_SLAB), lambda t: (t, 0, 0)),
            pl.BlockSpec((None, uv_cols, d), lambda t: (t // n_col, t % n_col, 0)),
        ],
        out_shape=[jax.ShapeDtypeStruct((depth, b, n), F32), table, table,
                   jax.ShapeDtypeStruct((depth, d_uv, d), BF16)],
        compiler_params=pltpu.CompilerParams(
            dimension_semantics=("arbitrary",),
            vmem_limit_bytes=VMEM_LIMIT_BYTES),
        name="prologue",
    )(c, w_ada, b_ada.reshape(depth, 1, n), positions.reshape(b, 1, s), freqs.reshape(-1, 1), w_in)


def _mix_in_kernel(x_ref, mod_ref, cos_ref, sin_ref, g_mix_ref, w_uvt_in_ref, w_in_ref, wst_ref,
                   bias_ref, g_q_ref, g_kv_ref, w_uq_ref, w_uk_ref, w_uvt_ref, v_one_ref,
                   g_og_ref, w_out_f_ref, w1_f_ref, w2_f_ref, yg_ref, q_ref, k_ref, vt_ref,
                   w_out_b_ref, w1_b_ref, w2_b_ref, *, d_gmlp, q_rank, kv_rank, q_scale, tm_sub):
    w_out_b_ref[...] = w_out_f_ref[...].astype(BF16)
    w1_b_ref[...] = w1_f_ref[...].astype(BF16)
    w2_b_ref[...] = w2_f_ref[...].astype(BF16)

    nt = (((1,), (1,)), ((), ()))
    mod = mod_ref[...]
    shift1, scale1 = mod[0:1], mod[1:2]
    gain1 = g_mix_ref[...] * (1.0 + scale1)
    n_chunk = tm_sub // CHUNK
    dg = d_gmlp // GMLP_GROUPS
    o_kv = q_rank
    o_kr = o_kv + kv_rank
    shift = HEAD_SLAB - QK_ROPE_DIM

    row = lax.broadcasted_iota(jnp.int32, (CHUNK, CHUNK), 0)
    col = lax.broadcasted_iota(jnp.int32, (CHUNK, CHUNK), 1)
    w_mix = [jnp.where(row <= col, wst_ref[g], 0.0).astype(BF16) for g in range(GMLP_GROUPS)]
    bias = [jnp.concatenate([bias_ref[g]] * n_chunk, axis=1) for g in range(GMLP_GROUPS)]

    n_sub = x_ref.shape[1] // tm_sub
    proj = []
    for sb in range(n_sub):
        rows = slice(sb * tm_sub, (sb + 1) * tm_sub)
        h = (_rms(x_ref[0, rows, :]) * gain1 + shift1).astype(BF16)
        proj.append((_dot(h, w_in_ref[...]),
                     lax.dot_general(w_uvt_in_ref[...], h, nt, preferred_element_type=F32)))

    def mla(sb):
        rows = slice(sb * tm_sub, (sb + 1) * tm_sub)
        z = proj[sb][0]
        cos_t = cos_ref[0, rows, :]
        sin_t = sin_ref[0, rows, :]
        c_q = (_rms(z[:, :o_kv]) * g_q_ref[...]).astype(BF16)
        qf = _dot(c_q, w_uq_ref[...])
        cos_q = cos_t * q_scale
        sin_q = sin_t * q_scale
        for hd in range(MLA_HEADS):
            qs = qf[:, hd * HEAD_SLAB:(hd + 1) * HEAD_SLAB]
            q_ref[0, hd, rows, :] = (qs * cos_q + pltpu.roll(qs, shift, 1) * sin_q).astype(BF16)

        c_kv = (_rms(z[:, o_kv:o_kr]) * g_kv_ref[...]).astype(BF16)
        kf = _dot(c_kv, w_uk_ref[...])
        kr = z[:, o_kr:]
        kr = kr * cos_t + pltpu.roll(kr, shift, 1) * sin_t
        for hd in range(MLA_HEADS):
            k_ref[0, hd, rows, :] = (kf[:, hd * HEAD_SLAB:(hd + 1) * HEAD_SLAB] + kr).astype(BF16)
        vt = lax.dot_general(w_uvt_ref[...], c_kv, nt,
                             preferred_element_type=F32)
        vt_ref[0, :, rows] = (vt + v_one_ref[...]).astype(BF16)

    def gmlp(sb):
        rows = slice(sb * tm_sub, (sb + 1) * tm_sub)
        uv_t = proj[sb][1]
        gu = _gelu_tanh(uv_t[:d_gmlp]).reshape(GMLP_GROUPS, dg, tm_sub)
        gv = _gelu_tanh(uv_t[d_gmlp:]).reshape(GMLP_GROUPS, dg, tm_sub)
        dv = gv - jnp.mean(gv, axis=1, keepdims=True)
        vn = (dv * lax.rsqrt(jnp.mean(dv * dv, axis=1, keepdims=True) + EPS)).astype(BF16)
        mixed = []
        for g in range(GMLP_GROUPS):
            lhs = jnp.concatenate([vn[g, :, c * CHUNK:(c + 1) * CHUNK] for c in range(n_chunk)], axis=0)
            out = _dot(lhs, w_mix[g])
            out = jnp.concatenate([out[c * dg:(c + 1) * dg] for c in range(n_chunk)], axis=1)
            mixed.append(out + bias[g])
        yg_t = (gu * jnp.stack(mixed)).reshape(d_gmlp, tm_sub)
        yg_t = yg_t * lax.rsqrt(jnp.mean(yg_t * yg_t, axis=0, keepdims=True) + EPS) * g_og_ref[...]
        yg_ref[0, rows, :] = yg_t.T.astype(BF16)

    for sb in range(n_sub):
        gmlp(sb)
    for sb in range(n_sub):
        mla(sb)


def _layer_spec(a, layer, **kw):
    return pl.BlockSpec((None,) + a.shape[1:], lambda i, j: (layer,) + (0,) * (a.ndim - 1), **kw)


def _mod_spec(mod, layer):
    return pl.BlockSpec((None, None) + mod.shape[2:], lambda i, j: (layer, i, 0, 0))


def _mix_in(layer, x, mod, cos_t, sin_t, g_mix, w_uvt_in, w_in, wst, bias, g_q, g_kv, w_uq, w_uk,
            w_uvt, v_one, g_og, w_out, w_ff1, w_ff2):
    b, s, d = x.shape
    n_j = s // TM_IN
    n_steps = b * n_j

    def slabbed(w):
        return w.reshape(w.shape[0], n_steps, w.shape[1] // n_steps, w.shape[2])

    def slab_in(w4):
        return pl.BlockSpec((None, None) + w4.shape[2:], lambda i, j: (layer, i * n_j + j, 0, 0))

    def slab_out(w4):
        return pl.BlockSpec((None,) + w4.shape[2:], lambda i, j: (i * n_j + j, 0, 0))

    f32_w = [slabbed(w) for w in (w_out, w_ff1, w_ff2)]
    d_gmlp = g_og.shape[1]
    q_rank = g_q.shape[-1]
    kv_rank = g_kv.shape[-1]
    tm = TM_IN
    q_scale = (QK_NOPE_DIM + QK_ROPE_DIM) ** -0.5 * math.log2(math.e)

    def par(a):
        return _layer_spec(a, layer)

    kern = functools.partial(_mix_in_kernel, d_gmlp=d_gmlp, q_rank=q_rank,
                             kv_rank=kv_rank, q_scale=q_scale, tm_sub=TM_SUB)
    slab = jax.ShapeDtypeStruct((b, MLA_HEADS, s, HEAD_SLAB), BF16)
    outs = pl.pallas_call(
        kern,
        grid=(b, n_j),
        in_specs=[
            pl.BlockSpec((1, tm, d), lambda i, j: (i, j, 0)),
            _mod_spec(mod, layer),
            pl.BlockSpec((1, tm, HEAD_SLAB), lambda i, j: (i, j, 0)),
            pl.BlockSpec((1, tm, HEAD_SLAB), lambda i, j: (i, j, 0)),
            par(g_mix), par(w_uvt_in), par(w_in), par(wst), par(bias),
            par(g_q), par(g_kv), par(w_uq), par(w_uk), par(w_uvt),
            pl.BlockSpec(v_one.shape, lambda i, j: (0, 0)), par(g_og),
        ] + [slab_in(w4) for w4 in f32_w],
        out_specs=[
            pl.BlockSpec((1, tm, d_gmlp), lambda i, j: (i, j, 0)),
            pl.BlockSpec((1, MLA_HEADS, tm, HEAD_SLAB), lambda i, j: (i, 0, j, 0)),
            pl.BlockSpec((1, MLA_HEADS, tm, HEAD_SLAB), lambda i, j: (i, 0, j, 0)),
            pl.BlockSpec((1, MLA_HEADS * V_SLAB, tm), lambda i, j: (i, 0, j)),
        ] + [slab_out(w4) for w4 in f32_w],
        out_shape=[
            jax.ShapeDtypeStruct((b, s, d_gmlp), BF16),
            slab, slab,
            jax.ShapeDtypeStruct((b, MLA_HEADS * V_SLAB, s), BF16),
        ] + [jax.ShapeDtypeStruct(w4.shape[1:], BF16) for w4 in f32_w],
        compiler_params=pltpu.CompilerParams(
            dimension_semantics=("arbitrary", "arbitrary"),
            vmem_limit_bytes=VMEM_LIMIT_BYTES),
        name="mix_in",
    )(x, mod, cos_t, sin_t, g_mix, w_uvt_in, w_in, wst, bias, g_q, g_kv, w_uq, w_uk, w_uvt, v_one,
      g_og, *f32_w)
    yg, q, k, vt = outs[:4]
    w_out_b, w1_b, w2_b = [o.reshape(w.shape[1:]) for o, w in zip(outs[4:], (w_out, w_ff1, w_ff2))]
    return yg, q, k, vt, w_out_b, w1_b, w2_b


def _attn_kernel(q_ref, k_ref, vt_ref, o_ref, *, tq):
    n_pair = q_ref.shape[1] // 2
    s_len = q_ref.shape[2]
    nt = (((1,), (1,)), ((), ()))
    half = tq // 2
    key = lax.broadcasted_iota(jnp.int32, (half, half), 0)
    qry = lax.broadcasted_iota(jnp.int32, (half, half), 1)
    causal = key <= qry
    n_blk = s_len // tq

    def qk(pair, c):
        q0, kmid, kend = c * tq, c * tq + half, (c + 1) * tq
        return [(lax.dot_general(k_ref[0, hd, :kmid, :], q_ref[0, hd, q0:kend, :], nt,
                                 preferred_element_type=F32),
                 lax.dot_general(k_ref[0, hd, kmid:kend, :], q_ref[0, hd, kmid:kend, :], nt,
                                 preferred_element_type=F32))
                for hd in (2 * pair, 2 * pair + 1)]

    jobs = [(pair, c) for pair in range(n_pair)
            for c in (range(n_blk) if pair % 2 == 0 else range(n_blk - 1, -1, -1))]
    nxt = qk(*jobs[0])
    for i, (pair, c) in enumerate(jobs):
        q0, kmid, kend = c * tq, c * tq + half, (c + 1) * tq
        scores = nxt
        if i + 1 < len(jobs):
            nxt = qk(*jobs[i + 1])
        halves = []
        for e in range(2):
            a, b = scores[e]
            top = jnp.concatenate([jnp.where(causal, a[q0:, :half], -1e30), a[q0:, half:]], axis=1)
            a = top if c == 0 else jnp.concatenate([a[:q0], top], axis=0)
            b = jnp.where(causal, b, -1e30)
            m = jnp.max(a, axis=0, keepdims=True)
            m = jnp.concatenate(
                [m[:, :half], jnp.maximum(m[:, half:], jnp.max(b, axis=0, keepdims=True))], axis=1)
            p_a = jnp.exp2(a - m).astype(BF16)
            p_b = jnp.exp2(b - m[:, half:]).astype(BF16)
            hd = 2 * pair + e
            vt = vt_ref.at[0, hd * V_SLAB:(hd + 1) * V_SLAB, :]
            o = _dot(vt[:, :kmid], p_a)
            o = jnp.concatenate([o[:, :half], o[:, half:] + _dot(vt[:, kmid:kend], p_b)], axis=1)
            halves.append(o[:V_HEAD_DIM] * (1.0 / o[V_HEAD_DIM:V_HEAD_DIM + 1]))
        o_ref[0, q0:kend, pair * LANES:(pair + 1) * LANES] = (
            jnp.concatenate(halves, axis=0).T.astype(o_ref.dtype))


def _attention(q, k, vt):
    b, hds, s, _ = q.shape
    hpb = ATTN_HEADS_PER_STEP
    return pl.pallas_call(
        functools.partial(_attn_kernel, tq=TQ),
        grid=(b, hds // hpb),
        in_specs=[
            pl.BlockSpec((1, hpb, s, HEAD_SLAB), lambda bi, j: (bi, j, 0, 0)),
            pl.BlockSpec((1, hpb, s, HEAD_SLAB), lambda bi, j: (bi, j, 0, 0)),
            pl.BlockSpec((1, hpb * V_SLAB, s), lambda bi, j: (bi, j, 0)),
        ],
        out_specs=pl.BlockSpec((1, s, hpb * V_HEAD_DIM), lambda bi, j: (bi, 0, j)),
        out_shape=jax.ShapeDtypeStruct((b, s, hds * V_HEAD_DIM), BF16),
        compiler_params=pltpu.CompilerParams(
            dimension_semantics=("arbitrary", "arbitrary"),
            vmem_limit_bytes=VMEM_LIMIT_BYTES),
        name="attn",
    )(q, k, vt)


def _mix_out_kernel(x_ref, yg_ref, ya_ref, mod_ref, g_oa_ref, w_out_ref, g_ffn_ref,
                    w1_ref, w2_ref, g_fin_ref, o_ref, *, final, tm_sub):
    mod = mod_ref[...]
    gate1, shift2, scale2, gate2 = mod[2:3], mod[3:4], mod[4:5], mod[5:6]
    gain2 = g_ffn_ref[...] * (1.0 + scale2)
    d_g = yg_ref.shape[-1]
    n_sub = x_ref.shape[1] // tm_sub

    mids = []
    for sb in range(n_sub):
        rows = slice(sb * tm_sub, (sb + 1) * tm_sub)
        ya = (_rms(ya_ref[0, rows, :].astype(F32)) * g_oa_ref[...]).astype(BF16)
        mixed = _dot(yg_ref[0, rows, :], w_out_ref[:d_g, :]) + _dot(ya, w_out_ref[d_g:, :])
        x1 = x_ref[0, rows, :] + gate1 * mixed
        mids.append((x1, (_rms(x1) * gain2 + shift2).astype(BF16)))

    for sb in range(n_sub):
        rows = slice(sb * tm_sub, (sb + 1) * tm_sub)
        x1, h = mids[sb]
        a = jnp.maximum(_dot(h, w1_ref[...]), 0.0)
        x2 = x1 + gate2 * _dot((a * a).astype(BF16), w2_ref[...])
        if final:
            x2 = _rms(x2) * g_fin_ref[...]
        o_ref[0, rows, :] = x2


def _mix_out(layer, x, yg, ya, mod, g_oa, w_out, g_ffn, w1, w2, g_fin, final):
    b, s, d = x.shape
    tm = TM_OUT

    def par(a):
        return _layer_spec(a, layer, pipeline_mode=pl.Buffered(1))

    def whole(a):
        return pl.BlockSpec(a.shape, lambda i, j: (0,) * a.ndim, pipeline_mode=pl.Buffered(1))

    def tok(a):
        return pl.BlockSpec((1, tm, a.shape[-1]), lambda i, j: (i, j, 0))

    return pl.pallas_call(
        functools.partial(_mix_out_kernel, final=final, tm_sub=TM_SUB),
        grid=(b, s // tm),
        in_specs=[
            tok(x), tok(yg), tok(ya),
            _mod_spec(mod, layer),
            par(g_oa), whole(w_out), par(g_ffn), whole(w1), whole(w2), whole(g_fin),
        ],
        out_specs=tok(x),
        out_shape=jax.ShapeDtypeStruct(x.shape, F32),
        compiler_params=pltpu.CompilerParams(
            dimension_semantics=("arbitrary", "arbitrary"),
            vmem_limit_bytes=VMEM_LIMIT_BYTES),
        name="mix_out_final" if final else "mix_out",
    )(x, yg, ya, mod, g_oa, w_out, g_ffn, w1, w2, g_fin)


def _rotate_half_cols(w):
    half = w.shape[-1] // 2
    return jnp.concatenate([-w[..., half:], w[..., :half]], axis=-1)


def _prep_weights(w_in, mla_w_uq, mla_w_ukv, d_gmlp):
    depth, d, _ = w_in.shape
    q_rank = mla_w_uq.shape[1]
    kv_rank = mla_w_ukv.shape[1]
    o_kr = w_in.shape[-1] - QK_ROPE_DIM
    kr = w_in[..., o_kr:]
    w_in_ext = jnp.concatenate(
        [w_in[..., 2 * d_gmlp:o_kr], jnp.zeros((depth, d, QK_NOPE_DIM), F32), kr,
         _rotate_half_cols(kr)], axis=-1).astype(BF16)

    wq = mla_w_uq.reshape(depth, q_rank, MLA_HEADS, QK_NOPE_DIM + QK_ROPE_DIM)
    rope = wq[..., QK_NOPE_DIM:]
    w_uq_ext = jnp.concatenate([wq, _rotate_half_cols(rope)], axis=-1)
    w_uq_ext = w_uq_ext.reshape(depth, q_rank, MLA_HEADS * HEAD_SLAB).astype(BF16)

    wkv = mla_w_ukv.reshape(depth, kv_rank, MLA_HEADS, QK_NOPE_DIM + V_HEAD_DIM)
    w_k = jnp.concatenate(
        [wkv[..., :QK_NOPE_DIM],
         jnp.zeros((depth, kv_rank, MLA_HEADS, HEAD_SLAB - QK_NOPE_DIM), F32)], axis=-1)
    w_uk_ext = w_k.reshape(depth, kv_rank, MLA_HEADS * HEAD_SLAB).astype(BF16)
    w_v = jnp.concatenate(
        [wkv[..., QK_NOPE_DIM:], jnp.zeros((depth, kv_rank, MLA_HEADS, BF16_ROWS), F32)], axis=-1)
    w_uvt = jnp.swapaxes(w_v.reshape(depth, kv_rank, MLA_HEADS * V_SLAB), 1, 2).astype(BF16)
    v_one = (jnp.arange(MLA_HEADS * V_SLAB) % V_SLAB == V_HEAD_DIM).astype(F32).reshape(-1, 1)
    return w_in_ext, w_uq_ext, w_uk_ext, w_uvt, v_one


def kernel(x, c, positions, w_ada, b_ada, norm_mix_g, w_in, gmlp_ws, gmlp_bs, mla_q_norm_g,
           mla_kv_norm_g, mla_w_uq, mla_w_ukv, out_norm_gmlp_g, out_norm_mla_g, w_out,
           norm_ffn_g, w_ff1, w_ff2, final_norm_g):
    b, s, d = x.shape
    depth = w_ada.shape[0]
    d_gmlp = out_norm_gmlp_g.shape[-1]

    mod, cos_t, sin_t, w_uvt_in = _prologue(c, w_ada, b_ada, positions, w_in, 2 * d_gmlp)
    mod = mod.reshape(depth, b, N_MOD, d)
    w_in_ext, w_uq_ext, w_uk_ext, w_uvt, v_one = _prep_weights(w_in, mla_w_uq, mla_w_ukv, d_gmlp)
    wst = jnp.swapaxes(gmlp_ws, 2, 3)
    bias = gmlp_bs[:, :, None, :]
    g_og = out_norm_gmlp_g[:, :, None]
    g_fin = final_norm_g.reshape(1, d)

    def rows(g):
        return g[:, None, :]

    for l in range(depth):
        yg, q, k, vt, w_out_b, w1_b, w2_b = _mix_in(
            l, x, mod, cos_t, sin_t, rows(norm_mix_g), w_uvt_in, w_in_ext, wst, bias,
            rows(mla_q_norm_g), rows(mla_kv_norm_g), w_uq_ext, w_uk_ext, w_uvt, v_one, g_og,
            w_out, w_ff1, w_ff2)
        ya = _attention(q, k, vt)
        x = _mix_out(l, x, yg, ya, mod, rows(out_norm_mla_g), w_out_b, rows(norm_ffn_g),
                     w1_b, w2_b, g_fin, final=(l == depth - 1))
    return x
```

```python
import functools
import math

import jax
import jax.numpy as jnp
from jax import lax
from jax.experimental import pallas as pl
from jax.experimental.pallas import tpu as pltpu

F32 = jnp.float32
BF16 = jnp.bfloat16

EPS = 1e-6
ROPE_THETA = 10000.0
N_MOD = 6

GMLP_GROUPS = 8
CHUNK = 128
MLA_HEADS = 8
QK_NOPE_DIM = 64
QK_ROPE_DIM = 32
V_HEAD_DIM = 64
HEAD_SLAB = 128
LANES = 128
BF16_ROWS = 16
V_SLAB = V_HEAD_DIM + BF16_ROWS

VMEM_LIMIT_BYTES = 56 * 1024 * 1024

TM_IN = 1024
TM_SUB = 512
TM_OUT = 1024
TQ = 512
ATTN_HEADS_PER_STEP = 4
ADA_TN = 1536


def _dot(a, b):
    return jnp.dot(a, b, preferred_element_type=F32)


def _rms(x):
    return x * lax.rsqrt(jnp.mean(x * x, axis=-1, keepdims=True) + EPS)


def _gelu_tanh(x):
    c = math.sqrt(2.0 / math.pi)
    half = 0.5 * x
    return half + half * jnp.tanh(x * (c + (c * 0.044715) * (x * x)))


def _prologue_kernel(c_ref, w_ref, b_ref, pos_ref, freq_ref, mod_ref, cos_ref, sin_ref):
    c = c_ref[...]
    c_act = (c / (1.0 + jnp.exp(-c))).astype(BF16)
    mod_ref[0] = _dot(c_act, w_ref[0].astype(BF16)) + b_ref[0]

    s_len = pos_ref.shape[2]
    pad = HEAD_SLAB - QK_NOPE_DIM - QK_ROPE_DIM
    for r in range(pos_ref.shape[0]):
        ang = freq_ref[...] * pos_ref[r].astype(F32)
        cs, sn = jnp.cos(ang), jnp.sin(ang)
        cos_ref[r] = jnp.concatenate(
            [jnp.ones((QK_NOPE_DIM, s_len), F32), cs, cs, jnp.zeros((pad, s_len), F32)], axis=0).T
        sin_ref[r] = jnp.concatenate(
            [jnp.zeros((QK_NOPE_DIM, s_len), F32), sn, sn, jnp.zeros((pad, s_len), F32)], axis=0).T


def _prologue(c, w_ada, b_ada, positions):
    depth, d, n = w_ada.shape
    b, s = positions.shape
    n_col = n // ADA_TN
    n_steps = depth * n_col
    rows = b // n_steps
    assert rows * n_steps == b
    freqs = ROPE_THETA ** (-jnp.arange(0, QK_ROPE_DIM, 2, dtype=F32) / QK_ROPE_DIM)
    table = jax.ShapeDtypeStruct((b, s, HEAD_SLAB), F32)
    return pl.pallas_call(
        _prologue_kernel,
        grid=(n_steps,),
        in_specs=[
            pl.BlockSpec((b, d), lambda t: (0, 0)),
            pl.BlockSpec((1, d, ADA_TN), lambda t: (t // n_col, 0, t % n_col)),
            pl.BlockSpec((1, 1, ADA_TN), lambda t: (t // n_col, 0, t % n_col)),
            pl.BlockSpec((rows, 1, s), lambda t: (t, 0, 0)),
            pl.BlockSpec((QK_ROPE_DIM // 2, 1), lambda t: (0, 0)),
        ],
        out_specs=[
            pl.BlockSpec((1, b, ADA_TN), lambda t: (t // n_col, 0, t % n_col)),
            pl.BlockSpec((rows, s, HEAD_SLAB), lambda t: (t, 0, 0)),
            pl.BlockSpec((rows, s, HEAD_SLAB), lambda t: (t, 0, 0)),
        ],
        out_shape=[jax.ShapeDtypeStruct((depth, b, n), F32), table, table],
        compiler_params=pltpu.CompilerParams(
            dimension_semantics=("arbitrary",),
            vmem_limit_bytes=VMEM_LIMIT_BYTES),
        name="prologue",
    )(c, w_ada, b_ada.reshape(depth, 1, n), positions.reshape(b, 1, s), freqs.reshape(-1, 1))


def _mix_in_kernel(x_ref, mod_ref, cos_ref, sin_ref, g_mix_ref, w_uvt_in_ref, w_in_ref, wst_ref,
                   bias_ref, g_q_ref, g_kv_ref, w_uq_ref, w_uk_ref, w_uvt_ref, v_one_ref,
                   g_og_ref, w_out_f_ref, w1_f_ref, w2_f_ref, yg_ref, q_ref, k_ref, vt_ref,
                   w_out_b_ref, w1_b_ref, w2_b_ref, *, d_gmlp, q_rank, kv_rank, q_scale, tm_sub,
                   pre_normed):
    w_out_b_ref[...] = w_out_f_ref[...].astype(BF16)
    w1_b_ref[...] = w1_f_ref[...].astype(BF16)
    w2_b_ref[...] = w2_f_ref[...].astype(BF16)

    nt = (((1,), (1,)), ((), ()))
    mod = mod_ref[...]
    shift1, scale1 = mod[0:1], mod[1:2]
    gain1 = g_mix_ref[...] * (1.0 + scale1)
    n_chunk = tm_sub // CHUNK
    dg = d_gmlp // GMLP_GROUPS
    o_kv = q_rank
    o_kr = o_kv + kv_rank
    shift = HEAD_SLAB - QK_ROPE_DIM

    row = lax.broadcasted_iota(jnp.int32, (CHUNK, CHUNK), 0)
    col = lax.broadcasted_iota(jnp.int32, (CHUNK, CHUNK), 1)
    w_mix = [jnp.where(row <= col, wst_ref[g], 0.0).astype(BF16) for g in range(GMLP_GROUPS)]
    bias = [jnp.concatenate([bias_ref[g]] * n_chunk, axis=1) for g in range(GMLP_GROUPS)]

    n_sub = x_ref.shape[1] // tm_sub
    proj = []
    for sb in range(n_sub):
        rows = slice(sb * tm_sub, (sb + 1) * tm_sub)
        if pre_normed:
            h = x_ref[0, rows, :]
        else:
            h = (_rms(x_ref[0, rows, :]) * gain1 + shift1).astype(BF16)
        proj.append((_dot(h, w_in_ref[...]),
                     lax.dot_general(w_uvt_in_ref[...], h, nt, preferred_element_type=F32)))

    def mla(sb):
        rows = slice(sb * tm_sub, (sb + 1) * tm_sub)
        z = proj[sb][0]
        cos_t = cos_ref[0, rows, :]
        sin_t = sin_ref[0, rows, :]
        c_q = (_rms(z[:, :o_kv]) * g_q_ref[...]).astype(BF16)
        qf = _dot(c_q, w_uq_ref[...])
        cos_q = cos_t * q_scale
        sin_q = sin_t * q_scale
        for hd in range(MLA_HEADS):
            qs = qf[:, hd * HEAD_SLAB:(hd + 1) * HEAD_SLAB]
            q_ref[0, hd, rows, :] = (qs * cos_q + pltpu.roll(qs, shift, 1) * sin_q).astype(BF16)

        c_kv = (_rms(z[:, o_kv:o_kr]) * g_kv_ref[...]).astype(BF16)
        kf = _dot(c_kv, w_uk_ref[...])
        kr = z[:, o_kr:]
        kr = kr * cos_t + pltpu.roll(kr, shift, 1) * sin_t
        for hd in range(MLA_HEADS):
            k_ref[0, hd, rows, :] = (kf[:, hd * HEAD_SLAB:(hd + 1) * HEAD_SLAB] + kr).astype(BF16)
        vt = lax.dot_general(w_uvt_ref[...], c_kv, nt,
                             preferred_element_type=F32)
        vt_ref[0, :, rows] = (vt + v_one_ref[...]).astype(BF16)

    def gmlp(sb):
        rows = slice(sb * tm_sub, (sb + 1) * tm_sub)
        uv_t = proj[sb][1]
        gu = _gelu_tanh(uv_t[:d_gmlp]).reshape(GMLP_GROUPS, dg, tm_sub)
        gv = _gelu_tanh(uv_t[d_gmlp:]).reshape(GMLP_GROUPS, dg, tm_sub)
        dv = gv - jnp.mean(gv, axis=1, keepdims=True)
        vn = (dv * lax.rsqrt(jnp.mean(dv * dv, axis=1, keepdims=True) + EPS)).astype(BF16)
        mixed = []
        for g in range(GMLP_GROUPS):
            lhs = jnp.concatenate([vn[g, :, c * CHUNK:(c + 1) * CHUNK] for c in range(n_chunk)], axis=0)
            out = _dot(lhs, w_mix[g])
            out = jnp.concatenate([out[c * dg:(c + 1) * dg] for c in range(n_chunk)], axis=1)
            mixed.append(out + bias[g])
        yg_t = (gu * jnp.stack(mixed)).reshape(d_gmlp, tm_sub)
        yg_t = yg_t * lax.rsqrt(jnp.mean(yg_t * yg_t, axis=0, keepdims=True) + EPS) * g_og_ref[...]
        yg_ref[0, rows, :] = yg_t.T.astype(BF16)

    for sb in range(n_sub):
        gmlp(sb)
    for sb in range(n_sub):
        mla(sb)


def _layer_spec(a, layer, **kw):
    return pl.BlockSpec((None,) + a.shape[1:], lambda i, j: (layer,) + (0,) * (a.ndim - 1), **kw)


def _mod_spec(mod, layer):
    return pl.BlockSpec((None, None) + mod.shape[2:], lambda i, j: (layer, i, 0, 0))


def _mix_in(layer, x, mod, cos_t, sin_t, g_mix, w_uvt_in, w_in, wst, bias, g_q, g_kv, w_uq, w_uk,
            w_uvt, v_one, g_og, w_out, w_ff1, w_ff2, pre_normed):
    b, s, d = x.shape
    n_j = s // TM_IN
    n_steps = b * n_j

    def slabbed(w):
        return w.reshape(w.shape[0], n_steps, w.shape[1] // n_steps, w.shape[2])

    def slab_in(w4):
        return pl.BlockSpec((None, None) + w4.shape[2:], lambda i, j: (layer, i * n_j + j, 0, 0))

    def slab_out(w4):
        return pl.BlockSpec((None,) + w4.shape[2:], lambda i, j: (i * n_j + j, 0, 0))

    f32_w = [slabbed(w) for w in (w_out, w_ff1, w_ff2)]
    d_gmlp = g_og.shape[1]
    q_rank = g_q.shape[-1]
    kv_rank = g_kv.shape[-1]
    tm = TM_IN
    q_scale = (QK_NOPE_DIM + QK_ROPE_DIM) ** -0.5 * math.log2(math.e)

    def par(a):
        return _layer_spec(a, layer)

    kern = functools.partial(_mix_in_kernel, d_gmlp=d_gmlp, q_rank=q_rank,
                             kv_rank=kv_rank, q_scale=q_scale, tm_sub=TM_SUB, pre_normed=pre_normed)
    slab = jax.ShapeDtypeStruct((b, MLA_HEADS, s, HEAD_SLAB), BF16)
    outs = pl.pallas_call(
        kern,
        grid=(b, n_j),
        in_specs=[
            pl.BlockSpec((1, tm, d), lambda i, j: (i, j, 0)),
            _mod_spec(mod, layer),
            pl.BlockSpec((1, tm, HEAD_SLAB), lambda i, j: (i, j, 0)),
            pl.BlockSpec((1, tm, HEAD_SLAB), lambda i, j: (i, j, 0)),
            par(g_mix), par(w_uvt_in), par(w_in), par(wst), par(bias),
            par(g_q), par(g_kv), par(w_uq), par(w_uk), par(w_uvt),
            pl.BlockSpec(v_one.shape, lambda i, j: (0, 0)), par(g_og),
        ] + [slab_in(w4) for w4 in f32_w],
        out_specs=[
            pl.BlockSpec((1, tm, d_gmlp), lambda i, j: (i, j, 0)),
            pl.BlockSpec((1, MLA_HEADS, tm, HEAD_SLAB), lambda i, j: (i, 0, j, 0)),
            pl.BlockSpec((1, MLA_HEADS, tm, HEAD_SLAB), lambda i, j: (i, 0, j, 0)),
            pl.BlockSpec((1, MLA_HEADS * V_SLAB, tm), lambda i, j: (i, 0, j)),
        ] + [slab_out(w4) for w4 in f32_w],
        out_shape=[
            jax.ShapeDtypeStruct((b, s, d_gmlp), BF16),
            slab, slab,
            jax.ShapeDtypeStruct((b, MLA_HEADS * V_SLAB, s), BF16),
        ] + [jax.ShapeDtypeStruct(w4.shape[1:], BF16) for w4 in f32_w],
        compiler_params=pltpu.CompilerParams(
            dimension_semantics=("arbitrary", "arbitrary"),
            vmem_limit_bytes=VMEM_LIMIT_BYTES),
        name="mix_in",
    )(x, mod, cos_t, sin_t, g_mix, w_uvt_in, w_in, wst, bias, g_q, g_kv, w_uq, w_uk, w_uvt, v_one,
      g_og, *f32_w)
    yg, q, k, vt = outs[:4]
    w_out_b, w1_b, w2_b = [o.reshape(w.shape[1:]) for o, w in zip(outs[4:], (w_out, w_ff1, w_ff2))]
    return yg, q, k, vt, w_out_b, w1_b, w2_b


def _attn_kernel(q_ref, k_ref, vt_ref, o_ref, *, tq):
    n_pair = q_ref.shape[1] // 2
    s_len = q_ref.shape[2]
    nt = (((1,), (1,)), ((), ()))
    half = tq // 2
    key = lax.broadcasted_iota(jnp.int32, (half, half), 0)
    qry = lax.broadcasted_iota(jnp.int32, (half, half), 1)
    causal = key <= qry
    n_blk = s_len // tq

    def qk(pair, c):
        q0, kmid, kend = c * tq, c * tq + half, (c + 1) * tq
        return [(lax.dot_general(k_ref[0, hd, :kmid, :], q_ref[0, hd, q0:kend, :], nt,
                                 preferred_element_type=F32),
                 lax.dot_general(k_ref[0, hd, kmid:kend, :], q_ref[0, hd, kmid:kend, :], nt,
                                 preferred_element_type=F32))
                for hd in (2 * pair, 2 * pair + 1)]

    jobs = [(pair, c) for pair in range(n_pair)
            for c in (range(n_blk) if pair % 2 == 0 else range(n_blk - 1, -1, -1))]
    nxt = qk(*jobs[0])
    for i, (pair, c) in enumerate(jobs):
        q0, kmid, kend = c * tq, c * tq + half, (c + 1) * tq
        scores = nxt
        if i + 1 < len(jobs):
            nxt = qk(*jobs[i + 1])
        halves = []
        for e in range(2):
            a, b = scores[e]
            top = jnp.concatenate([jnp.where(causal, a[q0:, :half], -1e30), a[q0:, half:]], axis=1)
            a = top if c == 0 else jnp.concatenate([a[:q0], top], axis=0)
            b = jnp.where(causal, b, -1e30)
            m = jnp.max(a, axis=0, keepdims=True)
            m = jnp.concatenate(
                [m[:, :half], jnp.maximum(m[:, half:], jnp.max(b, axis=0, keepdims=True))], axis=1)
            p_a = jnp.exp2(a - m).astype(BF16)
            p_b = jnp.exp2(b - m[:, half:]).astype(BF16)
            hd = 2 * pair + e
            vt = vt_ref.at[0, hd * V_SLAB:(hd + 1) * V_SLAB, :]
            o = _dot(vt[:, :kmid], p_a)
            o = jnp.concatenate([o[:, :half], o[:, half:] + _dot(vt[:, kmid:kend], p_b)], axis=1)
            halves.append(o[:V_HEAD_DIM] * (1.0 / o[V_HEAD_DIM:V_HEAD_DIM + 1]))
        o_ref[0, q0:kend, pair * LANES:(pair + 1) * LANES] = (
            jnp.concatenate(halves, axis=0).T.astype(o_ref.dtype))


def _attention(q, k, vt):
    b, hds, s, _ = q.shape
    hpb = ATTN_HEADS_PER_STEP
    return pl.pallas_call(
        functools.partial(_attn_kernel, tq=TQ),
        grid=(b, hds // hpb),
        in_specs=[
            pl.BlockSpec((1, hpb, s, HEAD_SLAB), lambda bi, j: (bi, j, 0, 0)),
            pl.BlockSpec((1, hpb, s, HEAD_SLAB), lambda bi, j: (bi, j, 0, 0)),
            pl.BlockSpec((1, hpb * V_SLAB, s), lambda bi, j: (bi, j, 0)),
        ],
        out_specs=pl.BlockSpec((1, s, hpb * V_HEAD_DIM), lambda bi, j: (bi, 0, j)),
        out_shape=jax.ShapeDtypeStruct((b, s, hds * V_HEAD_DIM), BF16),
        compiler_params=pltpu.CompilerParams(
            dimension_semantics=("arbitrary", "arbitrary"),
            vmem_limit_bytes=VMEM_LIMIT_BYTES),
        name="attn",
    )(q, k, vt)


def _mix_out_kernel(x_ref, yg_ref, ya_ref, mod_ref, g_oa_ref, w_out_ref, g_ffn_ref,
                    w1_ref, w2_ref, g_next_ref, mod_next_ref, o_ref, *maybe_h_ref, final, tm_sub):
    mod = mod_ref[...]
    gate1, shift2, scale2, gate2 = mod[2:3], mod[3:4], mod[4:5], mod[5:6]
    gain2 = g_ffn_ref[...] * (1.0 + scale2)
    d_g = yg_ref.shape[-1]
    n_sub = x_ref.shape[1] // tm_sub

    mids = []
    for sb in range(n_sub):
        rows = slice(sb * tm_sub, (sb + 1) * tm_sub)
        ya = (_rms(ya_ref[0, rows, :].astype(F32)) * g_oa_ref[...]).astype(BF16)
        mixed = _dot(yg_ref[0, rows, :], w_out_ref[:d_g, :]) + _dot(ya, w_out_ref[d_g:, :])
        x1 = x_ref[0, rows, :] + gate1 * mixed
        mids.append((x1, (_rms(x1) * gain2 + shift2).astype(BF16)))

    for sb in range(n_sub):
        rows = slice(sb * tm_sub, (sb + 1) * tm_sub)
        x1, h = mids[sb]
        a = jnp.maximum(_dot(h, w1_ref[...]), 0.0)
        x2 = x1 + gate2 * _dot((a * a).astype(BF16), w2_ref[...])
        if final:
            x2 = _rms(x2) * g_next_ref[...]
        else:
            mod_n = mod_next_ref[...]
            gain_n = g_next_ref[...] * (1.0 + mod_n[1:2])
            maybe_h_ref[0][0, rows, :] = (_rms(x2) * gain_n + mod_n[0:1]).astype(BF16)
        o_ref[0, rows, :] = x2


def _mix_out(layer, x, yg, ya, mod, g_oa, w_out, g_ffn, w1, w2, g_fin, g_mix, final):
    b, s, d = x.shape
    tm = TM_OUT

    def par(a):
        return _layer_spec(a, layer, pipeline_mode=pl.Buffered(1))

    def whole(a):
        return pl.BlockSpec(a.shape, lambda i, j: (0,) * a.ndim, pipeline_mode=pl.Buffered(1))

    def tok(a):
        return pl.BlockSpec((1, tm, a.shape[-1]), lambda i, j: (i, j, 0))

    nxt = layer if final else layer + 1
    g_next_spec = whole(g_fin) if final else _layer_spec(g_mix, nxt, pipeline_mode=pl.Buffered(1))
    outs = pl.pallas_call(
        functools.partial(_mix_out_kernel, final=final, tm_sub=TM_SUB),
        grid=(b, s // tm),
        in_specs=[
            tok(x), tok(yg), tok(ya),
            _mod_spec(mod, layer),
            par(g_oa), whole(w_out), par(g_ffn), whole(w1), whole(w2),
            g_next_spec, _mod_spec(mod, nxt),
        ],
        out_specs=[tok(x)] + ([] if final else [tok(x)]),
        out_shape=[jax.ShapeDtypeStruct(x.shape, F32)] + (
            [] if final else [jax.ShapeDtypeStruct(x.shape, BF16)]),
        compiler_params=pltpu.CompilerParams(
            dimension_semantics=("arbitrary", "arbitrary"),
            vmem_limit_bytes=VMEM_LIMIT_BYTES),
        name="mix_out_final" if final else "mix_out",
    )(x, yg, ya, mod, g_oa, w_out, g_ffn, w1, w2, g_fin if final else g_mix, mod)
    return (outs[0], None) if final else (outs[0], outs[1])


def _rotate_half_cols(w):
    half = w.shape[-1] // 2
    return jnp.concatenate([-w[..., half:], w[..., :half]], axis=-1)


def _prep_weights(w_in, mla_w_uq, mla_w_ukv, d_gmlp):
    depth, d, _ = w_in.shape
    q_rank = mla_w_uq.shape[1]
    kv_rank = mla_w_ukv.shape[1]
    o_kr = w_in.shape[-1] - QK_ROPE_DIM
    kr = w_in[..., o_kr:]
    w_uvt_in = jnp.swapaxes(w_in[..., :2 * d_gmlp], 1, 2).astype(BF16)
    w_in_ext = jnp.concatenate(
        [w_in[..., 2 * d_gmlp:o_kr], jnp.zeros((depth, d, QK_NOPE_DIM), F32), kr,
         _rotate_half_cols(kr)], axis=-1).astype(BF16)

    wq = mla_w_uq.reshape(depth, q_rank, MLA_HEADS, QK_NOPE_DIM + QK_ROPE_DIM)
    rope = wq[..., QK_NOPE_DIM:]
    w_uq_ext = jnp.concatenate([wq, _rotate_half_cols(rope)], axis=-1)
    w_uq_ext = w_uq_ext.reshape(depth, q_rank, MLA_HEADS * HEAD_SLAB).astype(BF16)

    wkv = mla_w_ukv.reshape(depth, kv_rank, MLA_HEADS, QK_NOPE_DIM + V_HEAD_DIM)
    w_k = jnp.concatenate(
        [wkv[..., :QK_NOPE_DIM],
         jnp.zeros((depth, kv_rank, MLA_HEADS, HEAD_SLAB - QK_NOPE_DIM), F32)], axis=-1)
    w_uk_ext = w_k.reshape(depth, kv_rank, MLA_HEADS * HEAD_SLAB).astype(BF16)
    w_v = jnp.concatenate(
        [wkv[..., QK_NOPE_DIM:], jnp.zeros((depth, kv_rank, MLA_HEADS, BF16_ROWS), F32)], axis=-1)
    w_uvt = jnp.swapaxes(w_v.reshape(depth, kv_rank, MLA_HEADS * V_SLAB), 1, 2).astype(BF16)
    v_one = (jnp.arange(MLA_HEADS * V_SLAB) % V_SLAB == V_HEAD_DIM).astype(F32).reshape(-1, 1)
    return w_uvt_in, w_in_ext, w_uq_ext, w_uk_ext, w_uvt, v_one


def kernel(x, c, positions, w_ada, b_ada, norm_mix_g, w_in, gmlp_ws, gmlp_bs, mla_q_norm_g,
           mla_kv_norm_g, mla_w_uq, mla_w_ukv, out_norm_gmlp_g, out_norm_mla_g, w_out,
           norm_ffn_g, w_ff1, w_ff2, final_norm_g):
    b, s, d = x.shape
    depth = w_ada.shape[0]
    d_gmlp = out_norm_gmlp_g.shape[-1]

    mod, cos_t, sin_t = _prologue(c, w_ada, b_ada, positions)
    mod = mod.reshape(depth, b, N_MOD, d)
    w_uvt_in, w_in_ext, w_uq_ext, w_uk_ext, w_uvt, v_one = _prep_weights(
        w_in, mla_w_uq, mla_w_ukv, d_gmlp)
    wst = jnp.swapaxes(gmlp_ws, 2, 3)
    bias = gmlp_bs[:, :, None, :]
    g_og = out_norm_gmlp_g[:, :, None]
    g_fin = final_norm_g.reshape(1, d)

    def rows(g):
        return g[:, None, :]

    h_next = None
    for l in range(depth):
        yg, q, k, vt, w_out_b, w1_b, w2_b = _mix_in(
            l, x if h_next is None else h_next, mod, cos_t, sin_t, rows(norm_mix_g), w_uvt_in,
            w_in_ext, wst, bias, rows(mla_q_norm_g), rows(mla_kv_norm_g), w_uq_ext, w_uk_ext, w_uvt,
            v_one, g_og, w_out, w_ff1, w_ff2, pre_normed=h_next is not None)
        ya = _attention(q, k, vt)
        x, h_next = _mix_out(l, x, yg, ya, mod, rows(out_norm_mla_g), w_out_b, rows(norm_ffn_g),
                             w1_b, w2_b, g_fin, rows(norm_mix_g), final=(l == depth - 1))
    return x
```

```python
import functools
import math

import jax
import jax.numpy as jnp
from jax import lax
from jax.experimental import pallas as pl
from jax.experimental.pallas import tpu as pltpu

F32 = jnp.float32
BF16 = jnp.bfloat16

EPS = 1e-6
ROPE_THETA = 10000.0
N_MOD = 6

GMLP_GROUPS = 8
CHUNK = 128
MLA_HEADS = 8
QK_NOPE_DIM = 64
QK_ROPE_DIM = 32
V_HEAD_DIM = 64
HEAD_SLAB = 128
LANES = 128
BF16_ROWS = 16
V_SLAB = V_HEAD_DIM + BF16_ROWS

VMEM_LIMIT_BYTES = 56 * 1024 * 1024

TM_IN = 1024
TM_SUB = 512
TM_OUT = 1024
TQ = 512
ATTN_HEADS_PER_STEP = 4
ADA_TN = 1536


def _dot(a, b):
    return jnp.dot(a, b, preferred_element_type=F32)


def _rms(x):
    return x * lax.rsqrt(jnp.mean(x * x, axis=-1, keepdims=True) + EPS)


def _gelu_tanh(x):
    c = math.sqrt(2.0 / math.pi)
    half = 0.5 * x
    return half + half * jnp.tanh(x * (c + (c * 0.044715) * (x * x)))


def _prologue_kernel(c_ref, w_ref, b_ref, pos_ref, freq_ref, mod_ref, cos_ref, sin_ref):
    c = c_ref[...]
    c_act = (c / (1.0 + jnp.exp(-c))).astype(BF16)
    mod_ref[0] = _dot(c_act, w_ref[0].astype(BF16)) + b_ref[0]

    s_len = pos_ref.shape[2]
    pad = HEAD_SLAB - QK_NOPE_DIM - QK_ROPE_DIM
    for r in range(pos_ref.shape[0]):
        ang = freq_ref[...] * pos_ref[r].astype(F32)
        cs, sn = jnp.cos(ang), jnp.sin(ang)
        cos_ref[r] = jnp.concatenate(
            [jnp.ones((QK_NOPE_DIM, s_len), F32), cs, cs, jnp.zeros((pad, s_len), F32)], axis=0).T
        sin_ref[r] = jnp.concatenate(
            [jnp.zeros((QK_NOPE_DIM, s_len), F32), sn, sn, jnp.zeros((pad, s_len), F32)], axis=0).T


def _prologue(c, w_ada, b_ada, positions):
    depth, d, n = w_ada.shape
    b, s = positions.shape
    n_col = n // ADA_TN
    n_steps = depth * n_col
    rows = b // n_steps
    assert rows * n_steps == b
    freqs = ROPE_THETA ** (-jnp.arange(0, QK_ROPE_DIM, 2, dtype=F32) / QK_ROPE_DIM)
    table = jax.ShapeDtypeStruct((b, s, HEAD_SLAB), F32)
    return pl.pallas_call(
        _prologue_kernel,
        grid=(n_steps,),
        in_specs=[
            pl.BlockSpec((b, d), lambda t: (0, 0)),
            pl.BlockSpec((1, d, ADA_TN), lambda t: (t // n_col, 0, t % n_col)),
            pl.BlockSpec((1, 1, ADA_TN), lambda t: (t // n_col, 0, t % n_col)),
            pl.BlockSpec((rows, 1, s), lambda t: (t, 0, 0)),
            pl.BlockSpec((QK_ROPE_DIM // 2, 1), lambda t: (0, 0)),
        ],
        out_specs=[
            pl.BlockSpec((1, b, ADA_TN), lambda t: (t // n_col, 0, t % n_col)),
            pl.BlockSpec((rows, s, HEAD_SLAB), lambda t: (t, 0, 0)),
            pl.BlockSpec((rows, s, HEAD_SLAB), lambda t: (t, 0, 0)),
        ],
        out_shape=[jax.ShapeDtypeStruct((depth, b, n), F32), table, table],
        compiler_params=pltpu.CompilerParams(
            dimension_semantics=("arbitrary",),
            vmem_limit_bytes=VMEM_LIMIT_BYTES),
        name="prologue",
    )(c, w_ada, b_ada.reshape(depth, 1, n), positions.reshape(b, 1, s), freqs.reshape(-1, 1))


def _mix_in_kernel(x_ref, mod_ref, cos_ref, sin_ref, g_mix_ref, w_uvt_in_ref, w_in_ref, wst_ref,
                   bias_ref, g_q_ref, g_kv_ref, w_uq_ref, w_uk_ref, w_uvt_ref, v_one_ref,
                   g_og_ref, w_out_f_ref, w1_f_ref, w2_f_ref, yg_ref, q_ref, k_ref, vt_ref,
                   w_out_b_ref, w1_b_ref, w2_b_ref, *, d_gmlp, q_rank, kv_rank, q_scale, tm_sub,
                   pre_normed):
    w_out_b_ref[...] = w_out_f_ref[...].astype(BF16)
    w1_b_ref[...] = w1_f_ref[...].astype(BF16)
    w2_b_ref[...] = w2_f_ref[...].astype(BF16)

    nt = (((1,), (1,)), ((), ()))
    mod = mod_ref[...]
    shift1, scale1 = mod[0:1], mod[1:2]
    gain1 = g_mix_ref[...] * (1.0 + scale1)
    n_chunk = tm_sub // CHUNK
    dg = d_gmlp // GMLP_GROUPS
    o_kv = q_rank
    o_kr = o_kv + kv_rank
    shift = HEAD_SLAB - QK_ROPE_DIM

    row = lax.broadcasted_iota(jnp.int32, (CHUNK, CHUNK), 0)
    col = lax.broadcasted_iota(jnp.int32, (CHUNK, CHUNK), 1)
    w_mix = [jnp.where(row <= col, wst_ref[g], 0.0).astype(BF16) for g in range(GMLP_GROUPS)]
    bias = [jnp.concatenate([bias_ref[g]] * n_chunk, axis=1) for g in range(GMLP_GROUPS)]

    n_sub = x_ref.shape[1] // tm_sub
    proj = []
    for sb in range(n_sub):
        rows = slice(sb * tm_sub, (sb + 1) * tm_sub)
        if pre_normed:
            h = x_ref[0, rows, :]
        else:
            h = (_rms(x_ref[0, rows, :]) * gain1 + shift1).astype(BF16)
        proj.append((_dot(h, w_in_ref[...]),
                     lax.dot_general(w_uvt_in_ref[...], h, nt, preferred_element_type=F32)))

    def mla(sb):
        rows = slice(sb * tm_sub, (sb + 1) * tm_sub)
        z = proj[sb][0]
        cos_t = cos_ref[0, rows, :]
        sin_t = sin_ref[0, rows, :]
        c_q = (_rms(z[:, :o_kv]) * g_q_ref[...]).astype(BF16)
        qf = _dot(c_q, w_uq_ref[...])
        cos_q = cos_t * q_scale
        sin_q = sin_t * q_scale
        for hd in range(MLA_HEADS):
            qs = qf[:, hd * HEAD_SLAB:(hd + 1) * HEAD_SLAB]
            q_ref[0, hd, rows, :] = (qs * cos_q + pltpu.roll(qs, shift, 1) * sin_q).astype(BF16)

        c_kv = (_rms(z[:, o_kv:o_kr]) * g_kv_ref[...]).astype(BF16)
        kf = _dot(c_kv, w_uk_ref[...])
        kr = z[:, o_kr:]
        kr = kr * cos_t + pltpu.roll(kr, shift, 1) * sin_t
        for hd in range(MLA_HEADS):
            k_ref[0, hd, rows, :] = (kf[:, hd * HEAD_SLAB:(hd + 1) * HEAD_SLAB] + kr).astype(BF16)
        vt = lax.dot_general(w_uvt_ref[...], c_kv, nt,
                             preferred_element_type=F32)
        vt_ref[0, :, rows] = (vt + v_one_ref[...]).astype(BF16)

    def gmlp(sb):
        rows = slice(sb * tm_sub, (sb + 1) * tm_sub)
        uv_t = proj[sb][1]
        gu = _gelu_tanh(uv_t[:d_gmlp]).reshape(GMLP_GROUPS, dg, tm_sub)
        gv = _gelu_tanh(uv_t[d_gmlp:]).reshape(GMLP_GROUPS, dg, tm_sub)
        dv = gv - jnp.mean(gv, axis=1, keepdims=True)
        vn = (dv * lax.rsqrt(jnp.mean(dv * dv, axis=1, keepdims=True) + EPS)).astype(BF16)
        mixed = []
        for g in range(GMLP_GROUPS):
            lhs = jnp.concatenate([vn[g, :, c * CHUNK:(c + 1) * CHUNK] for c in range(n_chunk)], axis=0)
            out = _dot(lhs, w_mix[g])
            out = jnp.concatenate([out[c * dg:(c + 1) * dg] for c in range(n_chunk)], axis=1)
            mixed.append(out + bias[g])
        yg_t = (gu * jnp.stack(mixed)).reshape(d_gmlp, tm_sub)
        yg_t = yg_t * lax.rsqrt(jnp.mean(yg_t * yg_t, axis=0, keepdims=True) + EPS) * g_og_ref[...]
        yg_ref[0, rows, :] = yg_t.T.astype(BF16)

    for sb in range(n_sub):
        gmlp(sb)
    for sb in range(n_sub):
        mla(sb)


def _layer_spec(a, layer, **kw):
    return pl.BlockSpec((None,) + a.shape[1:], lambda i, j: (layer,) + (0,) * (a.ndim - 1), **kw)


def _mod_spec(mod, layer):
    return pl.BlockSpec((None, None) + mod.shape[2:], lambda i, j: (layer, i, 0, 0))


def _mix_in(layer, x, mod, cos_t, sin_t, g_mix, w_uvt_in, w_in, wst, bias, g_q, g_kv, w_uq, w_uk,
            w_uvt, v_one, g_og, w_out, w_ff1, w_ff2, pre_normed):
    b, s, d = x.shape
    n_j = s // TM_IN
    n_steps = b * n_j

    def slabbed(w):
        return w.reshape(w.shape[0], n_steps, w.shape[1] // n_steps, w.shape[2])

    def slab_in(w4):
        return pl.BlockSpec((None, None) + w4.shape[2:], lambda i, j: (layer, i * n_j + j, 0, 0))

    def slab_out(w4):
        return pl.BlockSpec((None,) + w4.shape[2:], lambda i, j: (i * n_j + j, 0, 0))

    f32_w = [slabbed(w) for w in (w_out, w_ff1, w_ff2)]
    d_gmlp = g_og.shape[1]
    q_rank = g_q.shape[-1]
    kv_rank = g_kv.shape[-1]
    tm = TM_IN
    q_scale = (QK_NOPE_DIM + QK_ROPE_DIM) ** -0.5 * math.log2(math.e)

    def par(a):
        return _layer_spec(a, layer)

    kern = functools.partial(_mix_in_kernel, d_gmlp=d_gmlp, q_rank=q_rank,
                             kv_rank=kv_rank, q_scale=q_scale, tm_sub=TM_SUB, pre_normed=pre_normed)
    slab = jax.ShapeDtypeStruct((b, MLA_HEADS, s, HEAD_SLAB), BF16)
    outs = pl.pallas_call(
        kern,
        grid=(b, n_j),
        in_specs=[
            pl.BlockSpec((1, tm, d), lambda i, j: (i, j, 0)),
            _mod_spec(mod, layer),
            pl.BlockSpec((1, tm, HEAD_SLAB), lambda i, j: (i, j, 0)),
            pl.BlockSpec((1, tm, HEAD_SLAB), lambda i, j: (i, j, 0)),
            par(g_mix), par(w_uvt_in), par(w_in), par(wst), par(bias),
            par(g_q), par(g_kv), par(w_uq), par(w_uk), par(w_uvt),
            pl.BlockSpec(v_one.shape, lambda i, j: (0, 0)), par(g_og),
        ] + [slab_in(w4) for w4 in f32_w],
        out_specs=[
            pl.BlockSpec((1, tm, d_gmlp), lambda i, j: (i, j, 0)),
            pl.BlockSpec((1, MLA_HEADS, tm, HEAD_SLAB), lambda i, j: (i, 0, j, 0)),
            pl.BlockSpec((1, MLA_HEADS, tm, HEAD_SLAB), lambda i, j: (i, 0, j, 0)),
            pl.BlockSpec((1, MLA_HEADS * V_SLAB, tm), lambda i, j: (i, 0, j)),
        ] + [slab_out(w4) for w4 in f32_w],
        out_shape=[
            jax.ShapeDtypeStruct((b, s, d_gmlp), BF16),
            slab, slab,
            jax.ShapeDtypeStruct((b, MLA_HEADS * V_SLAB, s), BF16),
        ] + [jax.ShapeDtypeStruct(w4.shape[1:], BF16) for w4 in f32_w],
        compiler_params=pltpu.CompilerParams(
            dimension_semantics=("arbitrary", "arbitrary"),
            vmem_limit_bytes=VMEM_LIMIT_BYTES,
            allow_input_fusion=[i in (5, 6, 7, 11, 12, 13) for i in range(19)]),
        name="mix_in",
    )(x, mod, cos_t, sin_t, g_mix, w_uvt_in, w_in, wst, bias, g_q, g_kv, w_uq, w_uk, w_uvt, v_one,
      g_og, *f32_w)
    yg, q, k, vt = outs[:4]
    w_out_b, w1_b, w2_b = [o.reshape(w.shape[1:]) for o, w in zip(outs[4:], (w_out, w_ff1, w_ff2))]
    return yg, q, k, vt, w_out_b, w1_b, w2_b


def _attn_kernel(q_ref, k_ref, vt_ref, o_ref, *, tq):
    n_pair = q_ref.shape[1] // 2
    s_len = q_ref.shape[2]
    nt = (((1,), (1,)), ((), ()))
    half = tq // 2
    key = lax.broadcasted_iota(jnp.int32, (half, half), 0)
    qry = lax.broadcasted_iota(jnp.int32, (half, half), 1)
    causal = key <= qry
    n_blk = s_len // tq

    def qk(pair, c):
        q0, kmid, kend = c * tq, c * tq + half, (c + 1) * tq
        return [(lax.dot_general(k_ref[0, hd, :kmid, :], q_ref[0, hd, q0:kend, :], nt,
                                 preferred_element_type=F32),
                 lax.dot_general(k_ref[0, hd, kmid:kend, :], q_ref[0, hd, kmid:kend, :], nt,
                                 preferred_element_type=F32))
                for hd in (2 * pair, 2 * pair + 1)]

    jobs = [(pair, c) for pair in range(n_pair)
            for c in (range(n_blk) if pair % 2 == 0 else range(n_blk - 1, -1, -1))]
    nxt = qk(*jobs[0])
    for i, (pair, c) in enumerate(jobs):
        q0, kmid, kend = c * tq, c * tq + half, (c + 1) * tq
        scores = nxt
        if i + 1 < len(jobs):
            nxt = qk(*jobs[i + 1])
        halves = []
        for e in range(2):
            a, b = scores[e]
            top = jnp.concatenate([jnp.where(causal, a[q0:, :half], -1e30), a[q0:, half:]], axis=1)
            a = top if c == 0 else jnp.concatenate([a[:q0], top], axis=0)
            b = jnp.where(causal, b, -1e30)
            m = jnp.max(a, axis=0, keepdims=True)
            m = jnp.concatenate(
                [m[:, :half], jnp.maximum(m[:, half:], jnp.max(b, axis=0, keepdims=True))], axis=1)
            p_a = jnp.exp2(a - m).astype(BF16)
            p_b = jnp.exp2(b - m[:, half:]).astype(BF16)
            hd = 2 * pair + e
            vt = vt_ref.at[0, hd * V_SLAB:(hd + 1) * V_SLAB, :]
            o = _dot(vt[:, :kmid], p_a)
            o = jnp.concatenate([o[:, :half], o[:, half:] + _dot(vt[:, kmid:kend], p_b)], axis=1)
            halves.append(o[:V_HEAD_DIM] * (1.0 / o[V_HEAD_DIM:V_HEAD_DIM + 1]))
        o_ref[0, q0:kend, pair * LANES:(pair + 1) * LANES] = (
            jnp.concatenate(halves, axis=0).T.astype(o_ref.dtype))


def _attention(q, k, vt):
    b, hds, s, _ = q.shape
    hpb = ATTN_HEADS_PER_STEP
    return pl.pallas_call(
        functools.partial(_attn_kernel, tq=TQ),
        grid=(b, hds // hpb),
        in_specs=[
            pl.BlockSpec((1, hpb, s, HEAD_SLAB), lambda bi, j: (bi, j, 0, 0)),
            pl.BlockSpec((1, hpb, s, HEAD_SLAB), lambda bi, j: (bi, j, 0, 0)),
            pl.BlockSpec((1, hpb * V_SLAB, s), lambda bi, j: (bi, j, 0)),
        ],
        out_specs=pl.BlockSpec((1, s, hpb * V_HEAD_DIM), lambda bi, j: (bi, 0, j)),
        out_shape=jax.ShapeDtypeStruct((b, s, hds * V_HEAD_DIM), BF16),
        compiler_params=pltpu.CompilerParams(
            dimension_semantics=("arbitrary", "arbitrary"),
            vmem_limit_bytes=VMEM_LIMIT_BYTES),
        name="attn",
    )(q, k, vt)


def _mix_out_kernel(x_ref, yg_ref, ya_ref, mod_ref, g_oa_ref, w_out_ref, g_ffn_ref,
                    w1_ref, w2_ref, g_next_ref, mod_next_ref, o_ref, *maybe_h_ref, final, tm_sub):
    mod = mod_ref[...]
    gate1, shift2, scale2, gate2 = mod[2:3], mod[3:4], mod[4:5], mod[5:6]
    gain2 = g_ffn_ref[...] * (1.0 + scale2)
    d_g = yg_ref.shape[-1]
    n_sub = x_ref.shape[1] // tm_sub

    mids = []
    for sb in range(n_sub):
        rows = slice(sb * tm_sub, (sb + 1) * tm_sub)
        ya = (_rms(ya_ref[0, rows, :].astype(F32)) * g_oa_ref[...]).astype(BF16)
        mixed = _dot(yg_ref[0, rows, :], w_out_ref[:d_g, :]) + _dot(ya, w_out_ref[d_g:, :])
        x1 = x_ref[0, rows, :] + gate1 * mixed
        mids.append((x1, (_rms(x1) * gain2 + shift2).astype(BF16)))

    for sb in range(n_sub):
        rows = slice(sb * tm_sub, (sb + 1) * tm_sub)
        x1, h = mids[sb]
        a = jnp.maximum(_dot(h, w1_ref[...]), 0.0)
        x2 = x1 + gate2 * _dot((a * a).astype(BF16), w2_ref[...])
        if final:
            x2 = _rms(x2) * g_next_ref[...]
        else:
            mod_n = mod_next_ref[...]
            gain_n = g_next_ref[...] * (1.0 + mod_n[1:2])
            maybe_h_ref[0][0, rows, :] = (_rms(x2) * gain_n + mod_n[0:1]).astype(BF16)
        o_ref[0, rows, :] = x2


def _mix_out(layer, x, yg, ya, mod, g_oa, w_out, g_ffn, w1, w2, g_fin, g_mix, final):
    b, s, d = x.shape
    tm = TM_OUT

    def par(a):
        return _layer_spec(a, layer, pipeline_mode=pl.Buffered(1))

    def whole(a):
        return pl.BlockSpec(a.shape, lambda i, j: (0,) * a.ndim, pipeline_mode=pl.Buffered(1))

    def tok(a):
        return pl.BlockSpec((1, tm, a.shape[-1]), lambda i, j: (i, j, 0))

    nxt = layer if final else layer + 1
    g_next_spec = whole(g_fin) if final else _layer_spec(g_mix, nxt, pipeline_mode=pl.Buffered(1))
    outs = pl.pallas_call(
        functools.partial(_mix_out_kernel, final=final, tm_sub=TM_SUB),
        grid=(b, s // tm),
        in_specs=[
            tok(x), tok(yg), tok(ya),
            _mod_spec(mod, layer),
            par(g_oa), whole(w_out), par(g_ffn), whole(w1), whole(w2),
            g_next_spec, _mod_spec(mod, nxt),
        ],
        out_specs=[tok(x)] + ([] if final else [tok(x)]),
        out_shape=[jax.ShapeDtypeStruct(x.shape, F32)] + (
            [] if final else [jax.ShapeDtypeStruct(x.shape, BF16)]),
        compiler_params=pltpu.CompilerParams(
            dimension_semantics=("arbitrary", "arbitrary"),
            vmem_limit_bytes=VMEM_LIMIT_BYTES),
        name="mix_out_final" if final else "mix_out",
    )(x, yg, ya, mod, g_oa, w_out, g_ffn, w1, w2, g_fin if final else g_mix, mod)
    return (outs[0], None) if final else (outs[0], outs[1])


def _rotate_half_cols(w):
    half = w.shape[-1] // 2
    return jnp.concatenate([-w[..., half:], w[..., :half]], axis=-1)


def _prep_weights(w_in, mla_w_uq, mla_w_ukv, d_gmlp):
    depth, d, _ = w_in.shape
    q_rank = mla_w_uq.shape[1]
    kv_rank = mla_w_ukv.shape[1]
    o_kr = w_in.shape[-1] - QK_ROPE_DIM
    kr = w_in[..., o_kr:]
    w_uvt_in = jnp.swapaxes(w_in[..., :2 * d_gmlp], 1, 2).astype(BF16)
    w_in_ext = jnp.concatenate(
        [w_in[..., 2 * d_gmlp:o_kr], jnp.zeros((depth, d, QK_NOPE_DIM), F32), kr,
         _rotate_half_cols(kr)], axis=-1).astype(BF16)

    wq = mla_w_uq.reshape(depth, q_rank, MLA_HEADS, QK_NOPE_DIM + QK_ROPE_DIM)
    rope = wq[..., QK_NOPE_DIM:]
    w_uq_ext = jnp.concatenate([wq, _rotate_half_cols(rope)], axis=-1)
    w_uq_ext = w_uq_ext.reshape(depth, q_rank, MLA_HEADS * HEAD_SLAB).astype(BF16)

    wkv = mla_w_ukv.reshape(depth, kv_rank, MLA_HEADS, QK_NOPE_DIM + V_HEAD_DIM)
    w_k = jnp.concatenate(
        [wkv[..., :QK_NOPE_DIM],
         jnp.zeros((depth, kv_rank, MLA_HEADS, HEAD_SLAB - QK_NOPE_DIM), F32)], axis=-1)
    w_uk_ext = w_k.reshape(depth, kv_rank, MLA_HEADS * HEAD_SLAB).astype(BF16)
    w_v = jnp.concatenate(
        [wkv[..., QK_NOPE_DIM:], jnp.zeros((depth, kv_rank, MLA_HEADS, BF16_ROWS), F32)], axis=-1)
    w_uvt = jnp.swapaxes(w_v.reshape(depth, kv_rank, MLA_HEADS * V_SLAB), 1, 2).astype(BF16)
    v_one = (jnp.arange(MLA_HEADS * V_SLAB) % V_SLAB == V_HEAD_DIM).astype(F32).reshape(-1, 1)
    return w_uvt_in, w_in_ext, w_uq_ext, w_uk_ext, w_uvt, v_one


def kernel(x, c, positions, w_ada, b_ada, norm_mix_g, w_in, gmlp_ws, gmlp_bs, mla_q_norm_g,
           mla_kv_norm_g, mla_w_uq, mla_w_ukv, out_norm_gmlp_g, out_norm_mla_g, w_out,
           norm_ffn_g, w_ff1, w_ff2, final_norm_g):
    b, s, d = x.shape
    depth = w_ada.shape[0]
    d_gmlp = out_norm_gmlp_g.shape[-1]

    mod, cos_t, sin_t = _prologue(c, w_ada, b_ada, positions)
    mod = mod.reshape(depth, b, N_MOD, d)
    w_uvt_in, w_in_ext, w_uq_ext, w_uk_ext, w_uvt, v_one = _prep_weights(
        w_in, mla_w_uq, mla_w_ukv, d_gmlp)
    wst = jnp.swapaxes(gmlp_ws, 2, 3)
    bias = gmlp_bs[:, :, None, :]
    g_og = out_norm_gmlp_g[:, :, None]
    g_fin = final_norm_g.reshape(1, d)

    def rows(g):
        return g[:, None, :]

    h_next = None
    for l in range(depth):
        yg, q, k, vt, w_out_b, w1_b, w2_b = _mix_in(
            l, x if h_next is None else h_next, mod, cos_t, sin_t, rows(norm_mix_g), w_uvt_in,
            w_in_ext, wst, bias, rows(mla_q_norm_g), rows(mla_kv_norm_g), w_uq_ext, w_uk_ext, w_uvt,
            v_one, g_og, w_out, w_ff1, w_ff2, pre_normed=h_next is not None)
        ya = _attention(q, k, vt)
        x, h_next = _mix_out(l, x, yg, ya, mod, rows(out_norm_mla_g), w_out_b, rows(norm_ffn_g),
                             w1_b, w2_b, g_fin, rows(norm_mix_g), final=(l == depth - 1))
    return x
```
